```python
import jax, jax.numpy as jnp
from jax import lax
import numpy as np

D_MODEL = 1024
BATCH = 8
SEQ = 4096
DEPTH = 2

FOX_HEADS = 8
FOX_HEAD_DIM = 64
FOX_WIDTH = FOX_HEADS * FOX_HEAD_DIM
FOX_BLOCK = 128
MLSTM_HEADS = 4
MLSTM_HEAD_DIM = 128
MLSTM_WIDTH = MLSTM_HEADS * MLSTM_HEAD_DIM
MLSTM_CHUNK = 128
MLSTM_CONV = 4
SGU_GROUPS = 4
SGU_WIDTH = 512
SGU_GROUP_DIM = SGU_WIDTH // SGU_GROUPS
SGU_CHUNK = 128
N_BRANCH = 3
BRANCH_WIDTH = 512
N_GROUPS = 4
EXPERTS_PER_GROUP = 8
N_EXPERTS = N_GROUPS * EXPERTS_PER_GROUP
TOP_K = 2
D_EXPERT = 512
MOE_BLOCK = 128
DEEPNORM_ALPHA = (2 * DEPTH) ** 0.25
DEEPNORM_BETA = (8 * DEPTH) ** -0.25
LN_EPS = 1e-5
FORGET_BIAS = 3.0
NEG_INF = -1e30

IN_SIZES = (3 * FOX_WIDTH, FOX_HEADS, 2 * MLSTM_WIDTH, MLSTM_WIDTH, MLSTM_WIDTH,
            2 * MLSTM_HEADS, 2 * SGU_WIDTH, N_BRANCH * D_MODEL)
D_IN = sum(IN_SIZES)

kernel_name = "fox_mlstm_sgu_gated_hier_moe_deepnorm"


def layer_norm(x, g, b):
    xf = x.astype(jnp.float32)
    mu = xf.mean(-1, keepdims=True)
    var = jnp.square(xf - mu).mean(-1, keepdims=True)
    return ((xf - mu) * lax.rsqrt(var + LN_EPS) * g.astype(jnp.float32) + b.astype(jnp.float32)).astype(x.dtype)


def split_cols(h, sizes):
    idx = [int(c) for c in np.cumsum(sizes)[:-1]]
    return jnp.split(h, idx, axis=-1)


def causal_conv(x, w):
    kw = w.shape[0]
    s = x.shape[1]
    xp = jnp.pad(x, ((0, 0), (kw - 1, 0), (0, 0)))
    y = xp[:, 0:s] * w[0]
    for j in range(1, kw):
        y = y + xp[:, j:j + s] * w[j]
    return y


def fox_attention(q, k, v, log_f):
    b, s, h, dh = q.shape
    L = FOX_BLOCK
    nb = s // L
    f32 = jnp.float32
    F = jnp.cumsum(log_f.astype(f32), axis=1)

    def to_blocks(t):
        return t.reshape(b, nb, L, h, dh).transpose(1, 0, 3, 2, 4).astype(f32)

    qb = to_blocks(q) * (dh ** -0.5)
    kb = to_blocks(k)
    vb = to_blocks(v)
    Fb = F.reshape(b, nb, L, h).transpose(1, 0, 3, 2)
    pos = jnp.arange(L)

    def query_block(args):
        i, q_i, F_i = args

        def key_block(j, carry):
            m, l, acc = carry
            logits = jnp.einsum('bhqd,bhkd->bhqk', q_i, kb[j]) + F_i[..., :, None] - Fb[j][..., None, :]
            causal = (i * L + pos)[:, None] >= (j * L + pos)[None, :]
            logits = jnp.where(causal, logits, NEG_INF)
            m_new = jnp.maximum(m, logits.max(-1))
            p = jnp.exp(logits - m_new[..., None])
            corr = jnp.exp(m - m_new)
            return (m_new, l * corr + p.sum(-1),
                    acc * corr[..., None] + jnp.einsum('bhqk,bhkd->bhqd', p, vb[j]))

        init = (jnp.full((b, h, L), NEG_INF, f32), jnp.zeros((b, h, L), f32), jnp.zeros((b, h, L, dh), f32))
        m, l, acc = lax.fori_loop(0, i + 1, key_block, init)
        return acc / l[..., None]

    out = lax.map(query_block, (jnp.arange(nb), qb, Fb))
    return out.transpose(1, 0, 3, 2, 4).reshape(b, s, h * dh).astype(q.dtype)


def mlstm_chunkwise(q, k, v, i_pre, f_pre):
    b, s, h, dh = q.shape
    L = MLSTM_CHUNK
    nc = s // L
    f32 = jnp.float32

    def chunks(t):
        return t.reshape(b, nc, L, h, dh).transpose(1, 0, 3, 2, 4).astype(f32)

    def gchunks(t):
        return t.astype(f32).reshape(b, nc, L, h).transpose(1, 0, 3, 2)

    qc = chunks(q)
    kc = chunks(k) * (dh ** -0.5)
    vc = chunks(v)
    log_f = gchunks(jax.nn.log_sigmoid(f_pre.astype(f32)))
    log_i = gchunks(i_pre)
    causal = jnp.tril(jnp.ones((L, L), bool))

    def step(carry, xs):
        C, n, m = carry
        q_c, k_c, v_c, lf, li = xs
        bcum = jnp.cumsum(lf, axis=-1)
        D = jnp.where(causal, bcum[..., :, None] - bcum[..., None, :] + li[..., None, :], NEG_INF)
        inter = bcum + m[..., None]
        m_t = jnp.maximum(inter, D.max(-1))
        w_intra = jnp.exp(D - m_t[..., None])
        w_inter = jnp.exp(inter - m_t)
        qk = jnp.einsum('bhtd,bhsd->bhts', q_c, k_c) * w_intra
        num = jnp.einsum('bhts,bhse->bhte', qk, v_c) + w_inter[..., None] * jnp.einsum('bhtd,bhde->bhte', q_c, C)
        den = qk.sum(-1) + w_inter * jnp.einsum('bhtd,bhd->bht', q_c, n)
        h_c = num / jnp.maximum(jnp.abs(den), jnp.exp(-m_t))[..., None]
        b_last = bcum[..., -1]
        g = b_last[..., None] - bcum + li
        m_new = jnp.maximum(b_last + m, g.max(-1))
        decay = jnp.exp(b_last + m - m_new)
        w_k = jnp.exp(g - m_new[..., None])
        C_new = decay[..., None, None] * C + jnp.einsum('bhs,bhsd,bhse->bhde', w_k, k_c, v_c)
        n_new = decay[..., None] * n + jnp.einsum('bhs,bhsd->bhd', w_k, k_c)
        return (C_new, n_new, m_new), h_c

    init = (jnp.zeros((b, h, dh, dh), f32), jnp.zeros((b, h, dh), f32), jnp.zeros((b, h), f32))
    _, hs = lax.scan(step, init, (qc, kc, vc, log_f, log_i))
    return hs.transpose(1, 0, 3, 2, 4).reshape(b, s, h * dh).astype(q.dtype)


def spatial_gating(u, v, ln_g, ln_b, w_s, b_s):
    v = layer_norm(v, ln_g, ln_b)
    b, s, _ = v.shape
    nc = s // SGU_CHUNK
    vg = v.reshape(b, nc, SGU_CHUNK, SGU_GROUPS, SGU_GROUP_DIM)
    w = jnp.where(jnp.tril(jnp.ones((SGU_CHUNK, SGU_CHUNK), bool)), w_s, 0.0)
    z = jnp.einsum('gts,bnsgc->bntgc', w, vg) + b_s.T[None, None, :, :, None]
    return u * z.reshape(b, s, SGU_WIDTH)


def mixing_sublayer(x, w_in, b_in, conv_w, sgu_ln_g, sgu_ln_b, sgu_w, sgu_b, w_branch, w_out):
    b, s, _ = x.shape
    h = x @ w_in + b_in
    fox_qkv, fox_f, ml_qk, ml_v, ml_o, ml_if, sgu_uv, gate_pre = split_cols(h, IN_SIZES)
    fq, fk, fv = jnp.split(fox_qkv, 3, axis=-1)
    hs = (b, s, FOX_HEADS, FOX_HEAD_DIM)
    y_a = fox_attention(fq.reshape(hs), fk.reshape(hs), fv.reshape(hs),
                        jax.nn.log_sigmoid(fox_f.astype(jnp.float32)))
    mq, mk = jnp.split(jax.nn.silu(causal_conv(ml_qk, conv_w)), 2, axis=-1)
    mi, mf = jnp.split(ml_if, 2, axis=-1)
    ms = (b, s, MLSTM_HEADS, MLSTM_HEAD_DIM)
    y_b = jax.nn.sigmoid(ml_o) * mlstm_chunkwise(mq.reshape(ms), mk.reshape(ms), ml_v.reshape(ms), mi, mf)
    u, v = jnp.split(jax.nn.gelu(sgu_uv), 2, axis=-1)
    y_c = spatial_gating(u, v, sgu_ln_g, sgu_ln_b, sgu_w, sgu_b)
    branches = jnp.stack([y_a, y_b, y_c])
    proj = jnp.einsum('nbsc,ncd->nbsd', branches, w_branch)
    gates = jax.nn.sigmoid(gate_pre.reshape(b, s, N_BRANCH, D_MODEL))
    merged = jnp.einsum('nbsd,bsnd->bsd', proj, gates)
    return merged @ w_out


def hierarchical_moe(x, w_group, b_group, w_router, b_router, w_gate, w_up, w_down):
    b, s, d = x.shape
    t = x.reshape(-1, d)
    T = t.shape[0]
    f32 = jnp.float32
    logits_g = (t @ w_group + b_group).astype(f32)
    p_g = jax.nn.softmax(logits_g, axis=-1)
    g_top = jnp.argmax(logits_g, axis=-1)
    p_gsel = jnp.take_along_axis(p_g, g_top[:, None], axis=-1)
    logits_e = (t @ w_router + b_router).astype(f32).reshape(T, N_GROUPS, EXPERTS_PER_GROUP)
    logits_e = jnp.take_along_axis(logits_e, g_top[:, None, None], axis=1)[:, 0]
    p_e = jax.nn.softmax(logits_e, axis=-1)
    top_p, top_j = lax.top_k(p_e, TOP_K)
    gate = p_gsel * top_p / top_p.sum(-1, keepdims=True)
    expert = g_top[:, None] * EXPERTS_PER_GROUP + top_j
    A = T * TOP_K
    e_flat = expert.reshape(-1)
    tok = jnp.repeat(jnp.arange(T), TOP_K)
    gw = gate.reshape(-1)
    order = jnp.argsort(e_flat)
    e_sorted = e_flat[order]
    tok_sorted = tok[order]
    gw_sorted = gw[order]
    counts = jnp.bincount(e_flat, length=N_EXPERTS)
    padded = (counts + MOE_BLOCK - 1) // MOE_BLOCK * MOE_BLOCK
    start = jnp.cumsum(counts) - counts
    pend = jnp.cumsum(padded)
    pstart = pend - padded
    dest = pstart[e_sorted] + (jnp.arange(A) - start[e_sorted])
    n_blocks = -(-A // MOE_BLOCK) + N_EXPERTS
    rows = jnp.zeros((n_blocks * MOE_BLOCK, d), x.dtype).at[dest].set(t[tok_sorted])
    block_expert = jnp.minimum(jnp.searchsorted(pend, jnp.arange(n_blocks) * MOE_BLOCK, side='right'),
                               N_EXPERTS - 1)

    def expert_block(args):
        xb, e = args
        hid = jax.nn.silu(xb @ w_gate[e]) * (xb @ w_up[e])
        return hid @ w_down[e]

    y = lax.map(expert_block, (rows.reshape(n_blocks, MOE_BLOCK, d), block_expert)).reshape(-1, d)
    out = jnp.zeros_like(t).at[tok_sorted].add(gw_sorted[:, None].astype(t.dtype) * y[dest])
    return out.reshape(b, s, d)


def setup_inputs(seed: int = 0) -> dict:
    key = jax.random.key(seed)
    ks = jax.random.split(key, 24)
    f32 = jnp.float32
    nrm = lambda k, shape, scale: (jax.random.normal(k, shape, f32) * scale)
    L = DEPTH
    bias_offset = jnp.concatenate([
        jnp.zeros((3 * FOX_WIDTH,), f32), jnp.full((FOX_HEADS,), FORGET_BIAS, f32),
        jnp.zeros((4 * MLSTM_WIDTH + MLSTM_HEADS,), f32), jnp.full((MLSTM_HEADS,), FORGET_BIAS, f32),
        jnp.zeros((2 * SGU_WIDTH + N_BRANCH * D_MODEL,), f32)])
    return {
        "x": nrm(ks[0], (BATCH, SEQ, D_MODEL), 1.0),
        "ln_in_g": 1.0 + nrm(ks[1], (D_MODEL,), 0.01),
        "ln_in_b": nrm(ks[2], (D_MODEL,), 0.01),
        "w_in": nrm(ks[3], (L, D_MODEL, D_IN), D_MODEL ** -0.5),
        "b_in": nrm(ks[4], (L, D_IN), 0.02) + bias_offset,
        "conv_w": nrm(ks[5], (L, MLSTM_CONV, 2 * MLSTM_WIDTH), MLSTM_CONV ** -0.5),
        "sgu_ln_g": 1.0 + nrm(ks[6], (L, SGU_WIDTH), 0.01),
        "sgu_ln_b": nrm(ks[7], (L, SGU_WIDTH), 0.01),
        "sgu_w": nrm(ks[8], (L, SGU_GROUPS, SGU_CHUNK, SGU_CHUNK), SGU_CHUNK ** -0.5),
        "sgu_b": 1.0 + nrm(ks[9], (L, SGU_GROUPS, SGU_CHUNK), 0.01),
        "w_branch": nrm(ks[10], (L, N_BRANCH, BRANCH_WIDTH, D_MODEL), BRANCH_WIDTH ** -0.5),
        "w_out": nrm(ks[11], (L, D_MODEL, D_MODEL), DEEPNORM_BETA * D_MODEL ** -0.5),
        "ln1_g": 1.0 + nrm(ks[12], (L, D_MODEL), 0.01),
        "ln1_b": nrm(ks[13], (L, D_MODEL), 0.01),
        "w_group": nrm(ks[14], (L, D_MODEL, N_GROUPS), D_MODEL ** -0.5),
        "b_group": nrm(ks[15], (L, N_GROUPS), 0.01),
        "w_router": nrm(ks[16], (L, D_MODEL, N_EXPERTS), D_MODEL ** -0.5),
        "b_router": nrm(ks[17], (L, N_EXPERTS), 0.01),
        "w_gate": nrm(ks[18], (L, N_EXPERTS, D_MODEL, D_EXPERT), D_MODEL ** -0.5),
        "w_up": nrm(ks[19], (L, N_EXPERTS, D_MODEL, D_EXPERT), D_MODEL ** -0.5),
        "w_down": nrm(ks[20], (L, N_EXPERTS, D_EXPERT, D_MODEL), DEEPNORM_BETA * D_EXPERT ** -0.5),
        "ln2_g": 1.0 + nrm(ks[21], (L, D_MODEL), 0.01),
        "ln2_b": nrm(ks[22], (L, D_MODEL), 0.01),
    }


def reference(x, ln_in_g, ln_in_b, w_in, b_in, conv_w, sgu_ln_g, sgu_ln_b, sgu_w, sgu_b, w_branch, w_out,
              ln1_g, ln1_b, w_group, b_group, w_router, b_router, w_gate, w_up, w_down, ln2_g, ln2_b):
    x = layer_norm(x, ln_in_g, ln_in_b)
    for l in range(DEPTH):
        mix = mixing_sublayer(x, w_in[l], b_in[l], conv_w[l], sgu_ln_g[l], sgu_ln_b[l], sgu_w[l], sgu_b[l],
                              w_branch[l], w_out[l])
        x = layer_norm(DEEPNORM_ALPHA * x + mix, ln1_g[l], ln1_b[l])
        ffn = hierarchical_moe(x, w_group[l], b_group[l], w_router[l], b_router[l], w_gate[l], w_up[l], w_down[l])
        x = layer_norm(DEEPNORM_ALPHA * x + ffn, ln2_g[l], ln2_b[l])
    return x
```

```python
import functools

import jax
import jax.numpy as jnp
from jax import lax
from jax.experimental import pallas as pl
from jax.experimental.pallas import tpu as pltpu

F32 = jnp.float32
BF16 = jnp.bfloat16
U32 = jnp.uint32

D_MODEL = 1024
DEPTH = 2
FOX_HEADS = 8
FOX_HEAD_DIM = 64
FOX_WIDTH = FOX_HEADS * FOX_HEAD_DIM
MLSTM_HEADS = 4
MLSTM_HEAD_DIM = 128
MLSTM_WIDTH = MLSTM_HEADS * MLSTM_HEAD_DIM
MLSTM_CHUNK = 128
MLSTM_CONV = 4
SGU_GROUPS = 4
SGU_WIDTH = 512
SGU_CHUNK = 128
N_BRANCH = 3
N_GROUPS = 4
EXPERTS_PER_GROUP = 8
N_EXPERTS = N_GROUPS * EXPERTS_PER_GROUP
TOP_K = 2
D_EXPERT = 512
DEEPNORM_ALPHA = (2 * DEPTH) ** 0.25
LN_EPS = 1e-5
NEG_INF = -1e30

IN_SIZES = (3 * FOX_WIDTH, FOX_HEADS, 2 * MLSTM_WIDTH, MLSTM_WIDTH, MLSTM_WIDTH,
            2 * MLSTM_HEADS, 2 * SGU_WIDTH, N_BRANCH * D_MODEL)

LANES = 128
SUBLANES = 8
VMEM_LIMIT_BYTES = 56 * 1024 * 1024

SM_FOXF = 0
SM_MI = 8
SM_MF = 12
SM_ROWS = 16

RT_E1, RT_E2, RT_G1, RT_G2, RT_R1, RT_R2 = 0, 1, 2, 3, 4, 5
ROUTER_EXPERT_LANE0 = N_GROUPS

MOE_ROWS = 256
SGU_BLOCK = 256
CUMSUM_BLOCK = 256


def _params(sem, vmem=VMEM_LIMIT_BYTES):
    return pltpu.CompilerParams(dimension_semantics=sem, vmem_limit_bytes=vmem)


def _resident(shape, index_map):
    return pl.BlockSpec(shape, index_map, pipeline_mode=pl.Buffered(1))


def _layer_norm(x, g, b):
    mu = jnp.mean(x, axis=-1, keepdims=True)
    xc = x - mu
    var = jnp.mean(xc * xc, axis=-1, keepdims=True)
    return xc * lax.rsqrt(var + LN_EPS) * g + b


def _log_sigmoid(x):
    return jnp.minimum(x, 0.0) - jnp.log1p(jnp.exp(-jnp.abs(x)))


def _gelu_tanh(x):
    return 0.5 * x * (1.0 + jnp.tanh(0.7978845608028654 * (x + 0.044715 * x * x * x)))


def _dot(a, b):
    return jnp.dot(a, b, preferred_element_type=F32)


def _dot_nt(a, b):
    return lax.dot_general(a, b, (((1,), (1,)), ((), ())), preferred_element_type=F32)


def _dot_tn(a, b):
    return lax.dot_general(a, b, (((0,), (0,)), ((), ())), preferred_element_type=F32)


def _dot_f32(a, b):
    return jnp.dot(a, b, preferred_element_type=F32, precision=lax.Precision.HIGHEST)


def _pack_bf16_pair(a, b):
    ua = lax.bitcast_convert_type(a.astype(BF16).astype(F32), U32) >> 16
    ub = lax.bitcast_convert_type(b.astype(BF16).astype(F32), U32) & jnp.uint32(0xFFFF0000)
    return ua | ub


def _unpack_bf16_pair(w):
    a = lax.bitcast_convert_type(w << 16, F32)
    b = lax.bitcast_convert_type(w & jnp.uint32(0xFFFF0000), F32)
    return a, b


def _ln_kernel(x_ref, g_ref, b_ref, o_ref):
    o_ref[...] = _layer_norm(x_ref[...], g_ref[...], b_ref[...])


def _input_layer_norm(x, g, b, tm):
    T, D = x.shape
    return pl.pallas_call(
        _ln_kernel,
        out_shape=jax.ShapeDtypeStruct((T, D), F32),
        grid=(T // tm,),
        in_specs=[pl.BlockSpec((tm, D), lambda i: (i, 0)),
                  pl.BlockSpec((1, D), lambda i: (0, 0)),
                  pl.BlockSpec((1, D), lambda i: (0, 0))],
        out_specs=pl.BlockSpec((tm, D), lambda i: (i, 0)),
        compiler_params=_params(("parallel",)),
        name="ln_in",
    )(x, g.reshape(1, D), b.reshape(1, D))


C_FQ = 0
C_FV = C_FQ + FOX_WIDTH
C_MQK = C_FV + FOX_WIDTH
C_MV = C_MQK + 2 * MLSTM_WIDTH
C_MO = C_MV + MLSTM_WIDTH
C_SU = C_MO + MLSTM_WIDTH
C_SV = C_SU + SGU_WIDTH
C_GATE = C_SV + SGU_WIDTH
C_END = C_GATE + N_BRANCH * D_MODEL
PROJ_PIECE = 512


def _inproj_kernel(x_ref, wm_ref, bm_ref, wkt_ref, bkt_ref, ws_ref, bs_ref, wst_ref, bst_ref,
                   sg_ref, sb_ref, swt_ref, sbias_ref,
                   fq_ref, kt_ref, fv_ref, mqk_ref, mv_ref, mo_ref, yc_ref, gt_ref, sm_ref, smt_ref,
                   wbd_scr):
    tm = x_ref.shape[0]

    @pl.when(pl.program_id(0) == 0)
    def _():
        r = lax.broadcasted_iota(jnp.int32, (SGU_BLOCK, SGU_BLOCK), 0)
        c = lax.broadcasted_iota(jnp.int32, (SGU_BLOCK, SGU_BLOCK), 1)
        keep = jnp.logical_and(r // SGU_CHUNK == c // SGU_CHUNK, r >= c)
        for g in range(SGU_GROUPS):
            wbd_scr[g] = jnp.where(keep, swt_ref[g], 0.0).astype(BF16)

    xb = x_ref[...].astype(BF16)

    def proj(lo, width=PROJ_PIECE):
        return _dot(xb, wm_ref[:, lo:lo + width]) + bm_ref[:, lo:lo + width]

    fq_ref[...] = (proj(C_FQ) * (FOX_HEAD_DIM ** -0.5)).astype(BF16)
    fv_ref[...] = proj(C_FV).astype(BF16)
    kt_ref[...] = (_dot_nt(wkt_ref[...], xb) + bkt_ref[...]).astype(BF16)
    for j in range(2):
        mqk_ref[:, j * PROJ_PIECE:(j + 1) * PROJ_PIECE] = proj(C_MQK + j * PROJ_PIECE).astype(BF16)
    mv_ref[...] = proj(C_MV).astype(BF16)
    mo_ref[...] = jax.nn.sigmoid(proj(C_MO)).astype(BF16)
    for j in range(N_BRANCH * D_MODEL // PROJ_PIECE):
        gt_ref[:, j * PROJ_PIECE:(j + 1) * PROJ_PIECE] = jax.nn.sigmoid(
            proj(C_GATE + j * PROJ_PIECE)).astype(BF16)

    sm = _dot(xb, ws_ref[...]) + bs_ref[...]
    lane = lax.broadcasted_iota(jnp.int32, sm.shape, 1)
    is_ls = jnp.logical_or(lane < SM_MI, jnp.logical_and(lane >= SM_MF, lane < SM_ROWS))
    sm_ref[...] = jnp.where(is_ls, _log_sigmoid(sm), sm)
    smt = _dot_nt(wst_ref[...], xb) + bst_ref[...]
    row = lax.broadcasted_iota(jnp.int32, smt.shape, 0)
    is_ls_t = jnp.logical_or(row < SM_MI, row >= SM_MF)
    smt_ref[...] = jnp.where(is_ls_t, _log_sigmoid(smt), smt)

    u = _gelu_tanh(proj(C_SU))
    v = _gelu_tanh(proj(C_SV))
    vn = _layer_norm(v, sg_ref[...], sb_ref[...]).astype(BF16)
    for rb in range(tm // SGU_BLOCK):
        rows = slice(rb * SGU_BLOCK, (rb + 1) * SGU_BLOCK)
        for g in range(SGU_GROUPS):
            cols = slice(g * LANES, (g + 1) * LANES)
            z = _dot(wbd_scr[g], vn[rows, cols]) + sbias_ref[rows, cols]
            yc_ref[rows, cols] = (u[rows, cols] * z).astype(BF16)


def _in_projection(x, p, tm):
    T, D = x.shape
    row = lambda w: pl.BlockSpec((tm, w), lambda i: (i, 0))
    out_shapes = (
        jax.ShapeDtypeStruct((T, FOX_WIDTH), BF16),
        jax.ShapeDtypeStruct((FOX_WIDTH, T), BF16),
        jax.ShapeDtypeStruct((T, FOX_WIDTH), BF16),
        jax.ShapeDtypeStruct((T, 2 * MLSTM_WIDTH), BF16),
        jax.ShapeDtypeStruct((T, MLSTM_WIDTH), BF16),
        jax.ShapeDtypeStruct((T, MLSTM_WIDTH), BF16),
        jax.ShapeDtypeStruct((T, SGU_WIDTH), BF16),
        jax.ShapeDtypeStruct((T, N_BRANCH * D_MODEL), BF16),
        jax.ShapeDtypeStruct((T, LANES), F32),
        jax.ShapeDtypeStruct((SM_ROWS, T), F32),
    )
    out_specs = (row(FOX_WIDTH), pl.BlockSpec((FOX_WIDTH, tm), lambda i: (0, i)), row(FOX_WIDTH),
                 row(2 * MLSTM_WIDTH), row(MLSTM_WIDTH), row(MLSTM_WIDTH), row(SGU_WIDTH),
                 row(N_BRANCH * D_MODEL), row(LANES), pl.BlockSpec((SM_ROWS, tm), lambda i: (0, i)))
    const = lambda shape: _resident(shape, lambda i: tuple(0 for _ in shape))
    in_specs = [
        row(D),
        const((D, C_END)), const((1, C_END)),
        const((FOX_WIDTH, D)), const((FOX_WIDTH, 1)),
        const((D, LANES)), const((1, LANES)),
        const((SM_ROWS, D)), const((SM_ROWS, 1)),
        const((1, SGU_WIDTH)), const((1, SGU_WIDTH)),
        const((SGU_GROUPS, SGU_BLOCK, SGU_BLOCK)),
        const((tm, SGU_WIDTH)),
    ]
    return pl.pallas_call(
        _inproj_kernel,
        out_shape=out_shapes,
        grid=(T // tm,),
        in_specs=in_specs,
        out_specs=out_specs,
        scratch_shapes=[pltpu.VMEM((SGU_GROUPS, SGU_BLOCK, SGU_BLOCK), BF16)],
        compiler_params=_params(("arbitrary",)),
        name="in_proj",
    )(x, p["wm"], p["bm"], p["wkt"], p["bkt"], p["ws"], p["bs"], p["wst"], p["bst"],
      p["sgu_g"], p["sgu_b"], p["sgu_wt"], p["sgu_bias"])


def _fcum_kernel(sm_ref, smt_ref, fcol_ref, frow_ref):
    S = sm_ref.shape[0]
    cb = min(CUMSUM_BLOCK, S)
    r = lax.broadcasted_iota(jnp.int32, (cb, cb), 0)
    c = lax.broadcasted_iota(jnp.int32, (cb, cb), 1)
    lower = (r >= c).astype(F32)
    upper = (r <= c).astype(F32)
    carry_c = jnp.zeros((1, LANES), F32)
    carry_r = jnp.zeros((SM_ROWS, 1), F32)
    for blk in range(S // cb):
        sl = slice(blk * cb, (blk + 1) * cb)
        fc = _dot_f32(lower, sm_ref[sl, :]) + carry_c
        fcol_ref[sl, :] = fc
        carry_c = fc[cb - 1:cb, :]
        fr = _dot_f32(smt_ref[:, sl], upper) + carry_r
        frow_ref[:, sl] = fr
        carry_r = fr[:, cb - 1:cb]


def _forget_cumsum(sm, smt, B, S):
    T = B * S
    return pl.pallas_call(
        _fcum_kernel,
        out_shape=(jax.ShapeDtypeStruct((T, LANES), F32), jax.ShapeDtypeStruct((SM_ROWS, T), F32)),
        grid=(B,),
        in_specs=[pl.BlockSpec((S, LANES), lambda b: (b, 0)),
                  pl.BlockSpec((SM_ROWS, S), lambda b: (0, b))],
        out_specs=(pl.BlockSpec((S, LANES), lambda b: (b, 0)),
                   pl.BlockSpec((SM_ROWS, S), lambda b: (0, b))),
        compiler_params=_params(("parallel",)),
        name="forget_cumsum",
    )(sm, smt)


N_SPLIT = 3


def _split3(f):
    hi = f.astype(BF16).astype(F32)
    r1 = f - hi
    mid = r1.astype(BF16).astype(F32)
    lo = r1 - mid
    return hi, mid, lo


def _fox_kernel(q_ref, kt_ref, v_ref, fcol_ref, frow_ref, o_ref, kaug_scr, *, tq):
    S = v_ref.shape[0]
    pair = pl.program_id(1)
    qi = pl.program_id(2)

    @pl.when(qi == 0)
    def _():
        sub = lax.broadcasted_iota(jnp.int32, (LANES, S), 0)
        for hh in range(2):
            hi, mid, lo = _split3(frow_ref[hh])
            aug = jnp.where(sub < N_SPLIT, 1.0,
                            jnp.where(sub == N_SPLIT, -hi,
                                      jnp.where(sub == N_SPLIT + 1, -mid,
                                                jnp.where(sub == N_SPLIT + 2, -lo, 0.0))))
            kaug_scr[hh, 0:LANES, :] = kt_ref[...]
            kaug_scr[hh, LANES:2 * LANES, :] = aug.astype(BF16)

    q = q_ref[...]
    fcol = fcol_ref[...]
    lane = lax.broadcasted_iota(jnp.int32, (tq, LANES), 1)
    row = lax.broadcasted_iota(jnp.int32, (tq, tq), 0)
    col = lax.broadcasted_iota(jnp.int32, (tq, tq), 1)
    causal = row >= col

    outs = []
    for hh in range(2):
        head = 2 * pair + hh
        f = jnp.sum(jnp.where(lane == head, fcol, 0.0), axis=1, keepdims=True)
        hi, mid, lo = _split3(f)
        augq = jnp.where(lane == 0, hi,
                         jnp.where(lane == 1, mid,
                                   jnp.where(lane == 2, lo,
                                             jnp.where(lane < 2 * N_SPLIT, 1.0, 0.0))))
        in_head = (lane < FOX_HEAD_DIM) if hh == 0 else (lane >= FOX_HEAD_DIM)
        qm = jnp.where(in_head, q, jnp.zeros_like(q))
        lhs = jnp.concatenate([qm, augq.astype(BF16)], axis=1)

        def block(j, carry, masked, lhs=lhs, hh=hh):
            m, l, acc = carry
            off = pl.multiple_of(j * tq, tq)
            s = _dot(lhs, kaug_scr[hh, :, pl.ds(off, tq)])
            if masked:
                s = jnp.where(causal, s, NEG_INF)
            m_new = jnp.maximum(m, jnp.max(s, axis=1, keepdims=True))
            p = jnp.exp(s - m_new)
            corr = jnp.exp(m - m_new)
            l = l * corr + jnp.sum(p, axis=1, keepdims=True)
            acc = acc * corr + _dot(p.astype(BF16), v_ref[pl.ds(off, tq), :])
            return m_new, l, acc

        init = (jnp.full((tq, 1), NEG_INF, F32), jnp.zeros((tq, 1), F32), jnp.zeros((tq, LANES), F32))
        carry = lax.fori_loop(0, qi, functools.partial(block, masked=False), init)
        _, l, acc = block(qi, carry, True)
        outs.append(acc / l)

    o_ref[...] = jnp.where(lane < FOX_HEAD_DIM, outs[0], outs[1]).astype(o_ref.dtype)


def _fox_attention(fq, kt, fv, fcol, frow, B, S, tq):
    T = B * S
    nq = S // tq
    n_pairs = FOX_HEADS // 2
    frow3 = frow.reshape(SM_ROWS, 1, T)
    return pl.pallas_call(
        functools.partial(_fox_kernel, tq=tq),
        out_shape=jax.ShapeDtypeStruct((T, FOX_WIDTH), BF16),
        grid=(B, n_pairs, nq),
        in_specs=[pl.BlockSpec((tq, LANES), lambda b, p, i: (b * nq + i, p)),
                  pl.BlockSpec((LANES, S), lambda b, p, i: (p, b)),
                  pl.BlockSpec((S, LANES), lambda b, p, i: (b, p)),
                  pl.BlockSpec((tq, LANES), lambda b, p, i: (b * nq + i, 0)),
                  pl.BlockSpec((2, 1, S), lambda b, p, i: (p, 0, b))],
        out_specs=pl.BlockSpec((tq, LANES), lambda b, p, i: (b * nq + i, p)),
        scratch_shapes=[pltpu.VMEM((2, 2 * LANES, S), BF16)],
        compiler_params=_params(("parallel", "parallel", "arbitrary")),
        name="fox_attention",
    )(fq, kt, fv, fcol, frow3)


def _mlstm_kernel(qk_ref, v_ref, og_ref, sm_ref, smt_ref, cw_ref, y_ref, c_scr, n_scr, m_scr, tail_scr):
    L = MLSTM_CHUNK
    dh = MLSTM_HEAD_DIM

    @pl.when(pl.program_id(1) == 0)
    def _():
        c_scr[...] = jnp.zeros_like(c_scr)
        n_scr[...] = jnp.zeros_like(n_scr)
        m_scr[...] = jnp.zeros_like(m_scr)
        tail_scr[...] = jnp.zeros_like(tail_scr)

    x = qk_ref[...].astype(F32)
    xe = jnp.concatenate([tail_scr[...], x], axis=0)
    cw = cw_ref[...]
    y = x * cw[MLSTM_CONV - 1:MLSTM_CONV, :]
    for k in range(1, MLSTM_CONV):
        shifted = pltpu.roll(xe, k, 0)[SUBLANES:, :]
        y = y + shifted * cw[MLSTM_CONV - 1 - k:MLSTM_CONV - k, :]
    tail_scr[...] = x[L - SUBLANES:, :]
    y = y * jax.nn.sigmoid(y)
    q_all = y[:, :MLSTM_WIDTH]
    k_all = y[:, MLSTM_WIDTH:] * (dh ** -0.5)

    sm = sm_ref[...]
    smt = smt_ref[...]
    r = lax.broadcasted_iota(jnp.int32, (L, L), 0)
    c = lax.broadcasted_iota(jnp.int32, (L, L), 1)
    causal = r >= c
    bcol_all = _dot_f32(causal.astype(F32), sm)
    brow_all = _dot_f32(smt, (r <= c).astype(F32))

    for h in range(MLSTM_HEADS):
        hs = slice(h * dh, (h + 1) * dh)
        bq = bcol_all[:, SM_MF + h:SM_MF + h + 1]
        li_c = sm[:, SM_MI + h:SM_MI + h + 1]
        br = brow_all[SM_MF + h:SM_MF + h + 1, :]
        li_r = smt[SM_MI + h:SM_MI + h + 1, :]
        m_prev = m_scr[h:h + 1, 0:1]
        b_last = bq[L - 1:L, :]

        d = jnp.where(causal, bq - br + li_r, NEG_INF)
        inter = bq + m_prev
        m_t = jnp.maximum(inter, jnp.max(d, axis=1, keepdims=True))
        w_intra = jnp.exp(d - m_t)
        w_inter = jnp.exp(inter - m_t)

        qh = q_all[:, hs]
        qb = qh.astype(BF16)
        kh = k_all[:, hs]
        vh = v_ref[:, hs]
        qk = _dot_nt(qb, kh.astype(BF16)) * w_intra
        c_prev = c_scr[h]
        n_prev = n_scr[h:h + 1, :]
        num = _dot(qk.astype(BF16), vh) + w_inter * _dot(qb, c_prev.astype(BF16))
        den = jnp.sum(qk, axis=1, keepdims=True) + w_inter * jnp.sum(qh * n_prev, axis=1, keepdims=True)
        h_c = num / jnp.maximum(jnp.abs(den), jnp.exp(-m_t))
        y_ref[:, hs] = (og_ref[:, hs].astype(F32) * h_c).astype(y_ref.dtype)

        g_c = b_last - bq + li_c
        m_new = jnp.maximum(b_last + m_prev, jnp.max(g_c, axis=0, keepdims=True))
        decay = jnp.exp(b_last + m_prev - m_new)
        kw = kh * jnp.exp(g_c - m_new)
        c_scr[h] = decay * c_prev + _dot_tn(kw.astype(BF16), vh)
        n_scr[h:h + 1, :] = decay * n_prev + jnp.sum(kw, axis=0, keepdims=True)
        m_scr[h:h + 1, :] = jnp.broadcast_to(m_new, (1, LANES))


def _mlstm(mqk, mv, mo, sm, smt, conv_w, B, S):
    T = B * S
    L = MLSTM_CHUNK
    nc = S // L
    row = lambda w: pl.BlockSpec((L, w), lambda b, c: (b * nc + c, 0))
    return pl.pallas_call(
        _mlstm_kernel,
        out_shape=jax.ShapeDtypeStruct((T, MLSTM_WIDTH), BF16),
        grid=(B, nc),
        in_specs=[row(2 * MLSTM_WIDTH), row(MLSTM_WIDTH), row(MLSTM_WIDTH), row(LANES),
                  pl.BlockSpec((SM_ROWS, L), lambda b, c: (0, b * nc + c)),
                  pl.BlockSpec((MLSTM_CONV, 2 * MLSTM_WIDTH), lambda b, c: (0, 0))],
        out_specs=row(MLSTM_WIDTH),
        scratch_shapes=[pltpu.VMEM((MLSTM_HEADS, MLSTM_HEAD_DIM, MLSTM_HEAD_DIM), F32),
                        pltpu.VMEM((SUBLANES, MLSTM_HEAD_DIM), F32),
                        pltpu.VMEM((SUBLANES, LANES), F32),
                        pltpu.VMEM((SUBLANES, 2 * MLSTM_WIDTH), F32)],
        compiler_params=_params(("parallel", "arbitrary")),
        name="mlstm",
    )(mqk, mv, mo, sm, smt, conv_w)


def _merge_kernel(ya_ref, yb_ref, yc_ref, gt_ref, x_ref, wb_ref, wo_ref, g_ref, b_ref, wr_ref, br_ref,
                  x1_ref, x1p_ref, rt_ref, cnt_ref, carry_scr):
    tm = x_ref.shape[0]
    i = pl.program_id(0)

    @pl.when(i == 0)
    def _():
        carry_scr[...] = jnp.zeros_like(carry_scr)

    merged = None
    for n, y_ref in enumerate((ya_ref, yb_ref, yc_ref)):
        pr = _dot(y_ref[...], wb_ref[n]) * gt_ref[:, n * D_MODEL:(n + 1) * D_MODEL].astype(F32)
        merged = pr if merged is None else merged + pr
    mix = _dot(merged.astype(BF16), wo_ref[...])
    x1 = _layer_norm(DEEPNORM_ALPHA * x_ref[...] + mix, g_ref[...], b_ref[...])
    x1_ref[...] = x1
    half = D_MODEL // 2
    x1p_ref[...] = _pack_bf16_pair(x1[:, :half], x1[:, half:])

    logits = _dot(x1.astype(BF16), wr_ref[...]) + br_ref[...]
    lane = lax.broadcasted_iota(jnp.int32, (tm, LANES), 1)
    big = jnp.int32(LANES)
    glog = jnp.where(lane < N_GROUPS, logits, -jnp.inf)
    gmax = jnp.max(glog, axis=1, keepdims=True)
    g_top = jnp.min(jnp.where(glog == gmax, lane, big), axis=1, keepdims=True)
    p_g = 1.0 / jnp.sum(jnp.exp(glog - gmax), axis=1, keepdims=True)
    lo = ROUTER_EXPERT_LANE0 + EXPERTS_PER_GROUP * g_top
    el = jnp.where(jnp.logical_and(lane >= lo, lane < lo + EXPERTS_PER_GROUP), logits, -jnp.inf)
    m1 = jnp.max(el, axis=1, keepdims=True)
    i1 = jnp.min(jnp.where(el == m1, lane, big), axis=1, keepdims=True)
    el2 = jnp.where(lane == i1, -jnp.inf, el)
    m2 = jnp.max(el2, axis=1, keepdims=True)
    i2 = jnp.min(jnp.where(el2 == m2, lane, big), axis=1, keepdims=True)
    ratio = jnp.exp(m2 - m1)
    gate1 = p_g / (1.0 + ratio)
    gate2 = p_g * ratio / (1.0 + ratio)

    hit1 = lane == i1
    hit2 = lane == i2
    onehot = jnp.where(jnp.logical_or(hit1, hit2), 1.0, 0.0)
    r = lax.broadcasted_iota(jnp.int32, (tm, tm), 0)
    c = lax.broadcasted_iota(jnp.int32, (tm, tm), 1)
    before = jnp.where(r > c, 1.0, 0.0).astype(BF16)
    seen = _dot(before, onehot.astype(BF16)) + carry_scr[0:1, :]
    rank1 = jnp.sum(jnp.where(hit1, seen, 0.0), axis=1, keepdims=True)
    rank2 = jnp.sum(jnp.where(hit2, seen, 0.0), axis=1, keepdims=True)
    total = carry_scr[0:1, :] + jnp.sum(onehot, axis=0, keepdims=True)
    carry_scr[...] = jnp.broadcast_to(total, carry_scr.shape)
    cnt_ref[...] = jnp.broadcast_to(total, cnt_ref.shape)

    e1 = (i1 - ROUTER_EXPERT_LANE0).astype(F32)
    e2 = (i2 - ROUTER_EXPERT_LANE0).astype(F32)
    rec = jnp.zeros((tm, LANES), F32)
    for pos, val in ((RT_E1, e1), (RT_E2, e2), (RT_G1, gate1), (RT_G2, gate2), (RT_R1, rank1), (RT_R2, rank2)):
        rec = jnp.where(lane == pos, val, rec)
    rt_ref[...] = rec


def _merge_project_route(ya, yb, yc, gates, x, p, tm):
    T, D = x.shape
    row = lambda w: pl.BlockSpec((tm, w), lambda i: (i, 0))
    const = lambda shape: _resident(shape, lambda i: tuple(0 for _ in shape))
    return pl.pallas_call(
        _merge_kernel,
        out_shape=(jax.ShapeDtypeStruct((T, D), F32),
                   jax.ShapeDtypeStruct((T, D // 2), U32),
                   jax.ShapeDtypeStruct((T, LANES), F32),
                   jax.ShapeDtypeStruct((SUBLANES, LANES), F32)),
        grid=(T // tm,),
        in_specs=[row(FOX_WIDTH), row(MLSTM_WIDTH), row(SGU_WIDTH), row(N_BRANCH * D), row(D),
                  const((N_BRANCH, FOX_WIDTH, D)), const((D, D)), const((1, D)), const((1, D)),
                  const((D, LANES)), const((1, LANES))],
        out_specs=(row(D), row(D // 2), row(LANES), pl.BlockSpec((SUBLANES, LANES), lambda i: (0, 0))),
        scratch_shapes=[pltpu.VMEM((SUBLANES, LANES), F32)],
        compiler_params=_params(("arbitrary",)),
        name="merge_route",
    )(ya, yb, yc, gates, x, p["wb"], p["wo"], p["ln1_g"], p["ln1_b"], p["wr"], p["br"])


def _row_copy(src, src_row, dst, dst_row, sem):
    return pltpu.make_async_copy(src.at[pl.ds(src_row, 1)], dst.at[pl.ds(dst_row, 1)], sem)


def _dispatch_kernel(dest_ref, xp_ref, rows_in_ref, rows_ref, sem):
    del rows_in_ref
    td = xp_ref.shape[0]

    def issue(r, carry):
        for k in range(TOP_K):
            _row_copy(xp_ref, r, rows_ref, dest_ref[0, 0, TOP_K * r + k], sem.at[k]).start()
        return carry

    lax.fori_loop(0, td, issue, 0)

    def drain(r, carry):
        for k in range(TOP_K):
            _row_copy(xp_ref, r, rows_ref, 0, sem.at[k]).wait()
        return carry

    lax.fori_loop(0, td, drain, 0)


def _dispatch(x1p, dest, n_rows, td):
    T, W = x1p.shape
    dest3 = dest.reshape(T // td, 1, TOP_K * td)
    zeros = jnp.zeros((n_rows, W), U32)
    return pl.pallas_call(
        _dispatch_kernel,
        out_shape=jax.ShapeDtypeStruct((n_rows, W), U32),
        grid=(T // td,),
        in_specs=[pl.BlockSpec((1, 1, TOP_K * td), lambda i: (i, 0, 0), memory_space=pltpu.SMEM),
                  pl.BlockSpec((td, W), lambda i: (i, 0)),
                  pl.BlockSpec(memory_space=pl.ANY)],
        out_specs=pl.BlockSpec(memory_space=pl.ANY),
        scratch_shapes=[pltpu.SemaphoreType.DMA((TOP_K,))],
        input_output_aliases={2: 0},
        compiler_params=_params(("arbitrary",)),
        name="moe_dispatch",
    )(dest3, x1p, zeros)


def _expert_kernel(be_ref, nu_ref, rows_ref, wg_ref, wu_ref, wd_ref, y_ref, wg_scr, wu_scr, wd_scr):
    i = pl.program_id(0)
    used = i < nu_ref[0]
    new_expert = jnp.logical_or(i == 0, be_ref[i] != be_ref[jnp.maximum(i - 1, 0)])

    @pl.when(jnp.logical_and(used, new_expert))
    def _():
        wg_scr[...] = wg_ref[...].astype(BF16)
        wu_scr[...] = wu_ref[...].astype(BF16)
        wd_scr[...] = wd_ref[...].astype(BF16)

    @pl.when(used)
    def _():
        lo, hi = _unpack_bf16_pair(rows_ref[...])
        xb = jnp.concatenate([lo.astype(BF16), hi.astype(BF16)], axis=1)
        gate = _dot(xb, wg_scr[...])
        up = _dot(xb, wu_scr[...])
        hid = (gate * jax.nn.sigmoid(gate) * up).astype(BF16)
        y = _dot(hid, wd_scr[...])
        half = D_MODEL // 2
        y_ref[...] = _pack_bf16_pair(y[:, :half], y[:, half:])

    @pl.when(jnp.logical_not(used))
    def _():
        y_ref[...] = jnp.zeros_like(y_ref)


def _expert_mlp(rows, block_expert, n_used, w_gate, w_up, w_down):
    n_rows, W = rows.shape
    nb = n_rows // MOE_ROWS
    blk = lambda i, be, nu: (jnp.minimum(i, nu[0] - 1), 0)
    wsel = lambda i, be, nu: (be[i], 0, 0)
    return pl.pallas_call(
        _expert_kernel,
        out_shape=jax.ShapeDtypeStruct((n_rows, W), U32),
        grid_spec=pltpu.PrefetchScalarGridSpec(
            num_scalar_prefetch=2,
            grid=(nb,),
            in_specs=[pl.BlockSpec((MOE_ROWS, W), blk),
                      pl.BlockSpec((None, D_MODEL, D_EXPERT), wsel),
                      pl.BlockSpec((None, D_MODEL, D_EXPERT), wsel),
                      pl.BlockSpec((None, D_EXPERT, D_MODEL), wsel)],
            out_specs=pl.BlockSpec((MOE_ROWS, W), lambda i, be, nu: (i, 0)),
            scratch_shapes=[pltpu.VMEM((D_MODEL, D_EXPERT), BF16),
                            pltpu.VMEM((D_MODEL, D_EXPERT), BF16),
                            pltpu.VMEM((D_EXPERT, D_MODEL), BF16)]),
        compiler_params=_params(("arbitrary",)),
        name="moe_experts",
    )(block_expert, n_used, rows, w_gate, w_up, w_down)


def _combine_kernel(dest_ref, y_ref, rt_ref, x1_ref, g_ref, b_ref, o_ref, ybuf, sem):
    th = x1_ref.shape[0]
    half = D_MODEL // 2

    def issue(r, carry):
        for k in range(TOP_K):
            _row_copy(y_ref, dest_ref[0, 0, TOP_K * r + k], ybuf.at[k], r, sem.at[k]).start()
        return carry

    lax.fori_loop(0, th, issue, 0)

    def drain(r, carry):
        for k in range(TOP_K):
            _row_copy(y_ref, 0, ybuf.at[k], r, sem.at[k]).wait()
        return carry

    lax.fori_loop(0, th, drain, 0)

    rt = rt_ref[...]
    g1 = rt[:, RT_G1:RT_G1 + 1]
    g2 = rt[:, RT_G2:RT_G2 + 1]
    a_lo, a_hi = _unpack_bf16_pair(ybuf[0])
    b_lo, b_hi = _unpack_bf16_pair(ybuf[1])
    x1 = x1_ref[...]
    z_lo = DEEPNORM_ALPHA * x1[:, :half] + (g1 * a_lo + g2 * b_lo)
    z_hi = DEEPNORM_ALPHA * x1[:, half:] + (g1 * a_hi + g2 * b_hi)
    mu = (jnp.sum(z_lo, axis=1, keepdims=True) + jnp.sum(z_hi, axis=1, keepdims=True)) / D_MODEL
    c_lo = z_lo - mu
    c_hi = z_hi - mu
    var = (jnp.sum(c_lo * c_lo, axis=1, keepdims=True) + jnp.sum(c_hi * c_hi, axis=1, keepdims=True)) / D_MODEL
    inv = lax.rsqrt(var + LN_EPS)
    o_ref[:, :half] = c_lo * inv * g_ref[:, :half] + b_ref[:, :half]
    o_ref[:, half:] = c_hi * inv * g_ref[:, half:] + b_ref[:, half:]


def _combine(y, dest, route, x1, g, b, th):
    T, D = x1.shape
    W = y.shape[1]
    dest3 = dest.reshape(T // th, 1, TOP_K * th)
    return pl.pallas_call(
        _combine_kernel,
        out_shape=jax.ShapeDtypeStruct((T, D), F32),
        grid=(T // th,),
        in_specs=[pl.BlockSpec((1, 1, TOP_K * th), lambda i: (i, 0, 0), memory_space=pltpu.SMEM),
                  pl.BlockSpec(memory_space=pl.ANY),
                  pl.BlockSpec((th, LANES), lambda i: (i, 0)),
                  pl.BlockSpec((th, D), lambda i: (i, 0)),
                  pl.BlockSpec((1, D), lambda i: (0, 0)),
                  pl.BlockSpec((1, D), lambda i: (0, 0))],
        out_specs=pl.BlockSpec((th, D), lambda i: (i, 0)),
        scratch_shapes=[pltpu.VMEM((TOP_K, th, W), U32), pltpu.SemaphoreType.DMA((TOP_K,))],
        compiler_params=_params(("arbitrary",)),
        name="moe_combine",
    )(dest3, y, route, x1, g, b)


def _moe(x1, x1p, route, counts, p, tile):
    T = x1.shape[0]
    cnt = counts[0, ROUTER_EXPERT_LANE0:ROUTER_EXPERT_LANE0 + N_EXPERTS].astype(jnp.int32)
    padded = (cnt + MOE_ROWS - 1) // MOE_ROWS * MOE_ROWS
    pend = jnp.cumsum(padded)
    pstart = pend - padded
    expert = route[:, RT_E1:RT_E2 + 1].astype(jnp.int32)
    rank = route[:, RT_R1:RT_R2 + 1].astype(jnp.int32)
    dest = (pstart[expert] + rank).reshape(-1)
    nb = -(-(T * TOP_K) // MOE_ROWS) + N_EXPERTS
    block_start = jnp.arange(nb, dtype=jnp.int32) * MOE_ROWS
    block_expert = jnp.minimum(
        jnp.sum((pend[None, :] <= block_start[:, None]).astype(jnp.int32), axis=1), N_EXPERTS - 1)
    n_used = (pend[-1:] // MOE_ROWS).astype(jnp.int32)
    rows = _dispatch(x1p, dest, nb * MOE_ROWS, tile)
    y = _expert_mlp(rows, block_expert, n_used, p["w_gate"], p["w_up"], p["w_down"])
    return _combine(y, dest, route, x1, p["ln2_g"], p["ln2_b"], tile)


def _layer_params(l, w_in, b_in, conv_w, sgu_ln_g, sgu_ln_b, sgu_w, sgu_b, w_branch, w_out, ln1_g, ln1_b,
                  w_group, b_group, w_router, b_router, w_gate, w_up, w_down, ln2_g, ln2_b, tm):
    offs = [0]
    for s in IN_SIZES:
        offs.append(offs[-1] + s)
    w, b = w_in[l], b_in[l]
    seg = lambda a, i: a[..., offs[i]:offs[i + 1]]
    fox_w, fox_b = seg(w, 0), seg(b, 0)
    fq_w, fk_w, fv_w = (fox_w[:, j * FOX_WIDTH:(j + 1) * FOX_WIDTH] for j in range(3))
    fq_b, fk_b, fv_b = (fox_b[j * FOX_WIDTH:(j + 1) * FOX_WIDTH] for j in range(3))
    wm = jnp.concatenate([fq_w, fv_w, seg(w, 2), seg(w, 3), seg(w, 4), seg(w, 6), seg(w, 7)], axis=1)
    bm = jnp.concatenate([fq_b, fv_b, seg(b, 2), seg(b, 3), seg(b, 4), seg(b, 6), seg(b, 7)])
    n_small = IN_SIZES[1] + IN_SIZES[5]
    ws = jnp.concatenate([seg(w, 1), seg(w, 5), jnp.zeros((D_MODEL, LANES - n_small), F32)], axis=1)
    bs = jnp.concatenate([seg(b, 1), seg(b, 5), jnp.zeros((LANES - n_small,), F32)])
    reps = SGU_BLOCK // SGU_CHUNK
    sgu_bias = jnp.broadcast_to(sgu_b[l].T[:, :, None], (SGU_CHUNK, SGU_GROUPS, SGU_WIDTH // SGU_GROUPS))
    sgu_bias = jnp.tile(sgu_bias.reshape(SGU_CHUNK, SGU_WIDTH), (tm // SGU_CHUNK, 1))
    n_route = N_GROUPS + N_EXPERTS
    wr = jnp.concatenate([w_group[l], w_router[l], jnp.zeros((D_MODEL, LANES - n_route), F32)], axis=1)
    br = jnp.concatenate([b_group[l], b_router[l], jnp.zeros((LANES - n_route,), F32)])
    return {
        "wm": wm.astype(BF16), "bm": bm.reshape(1, -1),
        "wkt": fk_w.T.astype(BF16), "bkt": fk_b.reshape(-1, 1),
        "ws": ws.astype(BF16), "bs": bs.reshape(1, -1),
        "wst": ws[:, :SM_ROWS].T.astype(BF16), "bst": bs[:SM_ROWS].reshape(-1, 1),
        "conv_w": conv_w[l],
        "sgu_g": sgu_ln_g[l].reshape(1, -1), "sgu_b": sgu_ln_b[l].reshape(1, -1),
        "sgu_wt": jnp.tile(sgu_w[l], (1, reps, reps)), "sgu_bias": sgu_bias,
        "wb": w_branch[l].astype(BF16), "wo": w_out[l].astype(BF16),
        "ln1_g": ln1_g[l].reshape(1, -1), "ln1_b": ln1_b[l].reshape(1, -1),
        "wr": wr.astype(BF16), "br": br.reshape(1, -1),
        "w_gate": w_gate[l], "w_up": w_up[l], "w_down": w_down[l],
        "ln2_g": ln2_g[l].reshape(1, -1), "ln2_b": ln2_b[l].reshape(1, -1),
    }


def _tiles(B, S):
    T = B * S
    tm = 512 if T % 512 == 0 else 256
    tq = 512 if S % 512 == 0 else 256
    return tm, tq


def kernel(x, ln_in_g, ln_in_b, w_in, b_in, conv_w, sgu_ln_g, sgu_ln_b, sgu_w, sgu_b, w_branch, w_out,
           ln1_g, ln1_b, w_group, b_group, w_router, b_router, w_gate, w_up, w_down, ln2_g, ln2_b):
    B, S, D = x.shape
    assert D == D_MODEL and S % SGU_BLOCK == 0
    T = B * S
    tm, tq = _tiles(B, S)
    h = _input_layer_norm(x.reshape(T, D), ln_in_g, ln_in_b, tm)
    for l in range(DEPTH):
        p = _layer_params(l, w_in, b_in, conv_w, sgu_ln_g, sgu_ln_b, sgu_w, sgu_b, w_branch, w_out, ln1_g,
                          ln1_b, w_group, b_group, w_router, b_router, w_gate, w_up, w_down, ln2_g, ln2_b, tm)
        fq, kt, fv, mqk, mv, mo, yc, gates, sm, smt = _in_projection(h, p, tm)
        fcol, frow = _forget_cumsum(sm, smt, B, S)
        ya = _fox_attention(fq, kt, fv, fcol, frow, B, S, tq)
        yb = _mlstm(mqk, mv, mo, sm, smt, p["conv_w"], B, S)
        x1, x1p, route, counts = _merge_project_route(ya, yb, yc, gates, h, p, tm)
        h = _moe(x1, x1p, route, counts, p, 256)
    return h.reshape(B, S, D)
```

```python
import functools

import jax
import jax.numpy as jnp
from jax import lax
from jax.experimental import pallas as pl
from jax.experimental.pallas import tpu as pltpu

F32 = jnp.float32
BF16 = jnp.bfloat16

D_MODEL = 1024
DEPTH = 2
FOX_HEADS = 8
FOX_HEAD_DIM = 64
FOX_WIDTH = FOX_HEADS * FOX_HEAD_DIM
MLSTM_HEADS = 4
MLSTM_HEAD_DIM = 128
MLSTM_WIDTH = MLSTM_HEADS * MLSTM_HEAD_DIM
MLSTM_CHUNK = 128
MLSTM_CONV = 4
SGU_GROUPS = 4
SGU_WIDTH = 512
SGU_CHUNK = 128
N_BRANCH = 3
N_GROUPS = 4
EXPERTS_PER_GROUP = 8
N_EXPERTS = N_GROUPS * EXPERTS_PER_GROUP
TOP_K = 2
D_EXPERT = 512
DEEPNORM_ALPHA = (2 * DEPTH) ** 0.25
LN_EPS = 1e-5
NEG_INF = -1e30

IN_SIZES = (3 * FOX_WIDTH, FOX_HEADS, 2 * MLSTM_WIDTH, MLSTM_WIDTH, MLSTM_WIDTH,
            2 * MLSTM_HEADS, 2 * SGU_WIDTH, N_BRANCH * D_MODEL)

LANES = 128
SUBLANES = 8
VMEM_LIMIT_BYTES = 56 * 1024 * 1024

SM_FOXF = 0
SM_MI = 8
SM_MF = 12
SM_ROWS = 16

RT_E1, RT_E2, RT_G1, RT_G2, RT_R1, RT_R2 = 0, 1, 2, 3, 4, 5
RT_WIDTH = 8
ROUTER_EXPERT_LANE0 = N_GROUPS

MOE_ROWS = 256
SGU_BLOCK = 256
CUMSUM_BLOCK = 256


def _params(sem, vmem=VMEM_LIMIT_BYTES):
    return pltpu.CompilerParams(dimension_semantics=sem, vmem_limit_bytes=vmem)


def _resident(shape, index_map):
    return pl.BlockSpec(shape, index_map, pipeline_mode=pl.Buffered(1))


def _layer_norm(x, g, b):
    mu = jnp.mean(x, axis=-1, keepdims=True)
    xc = x - mu
    var = jnp.mean(xc * xc, axis=-1, keepdims=True)
    return xc * lax.rsqrt(var + LN_EPS) * g + b


def _log_sigmoid(x):
    return jnp.minimum(x, 0.0) - jnp.log1p(jnp.exp(-jnp.abs(x)))


def _gelu_tanh(x):
    return 0.5 * x * (1.0 + jnp.tanh(0.7978845608028654 * (x + 0.044715 * x * x * x)))


def _dot(a, b):
    return jnp.dot(a, b, preferred_element_type=F32)


def _dot_nt(a, b):
    return lax.dot_general(a, b, (((1,), (1,)), ((), ())), preferred_element_type=F32)


def _dot_tn(a, b):
    return lax.dot_general(a, b, (((0,), (0,)), ((), ())), preferred_element_type=F32)


def _dot_f32(a, b):
    return jnp.dot(a, b, preferred_element_type=F32, precision=lax.Precision.HIGHEST)


def _ln_kernel(x_ref, g_ref, b_ref, o_ref):
    o_ref[...] = _layer_norm(x_ref[...], g_ref[...], b_ref[...])


def _input_layer_norm(x, g, b, tm):
    T, D = x.shape
    return pl.pallas_call(
        _ln_kernel,
        out_shape=jax.ShapeDtypeStruct((T, D), F32),
        grid=(T // tm,),
        in_specs=[pl.BlockSpec((tm, D), lambda i: (i, 0)),
                  pl.BlockSpec((1, D), lambda i: (0, 0)),
                  pl.BlockSpec((1, D), lambda i: (0, 0))],
        out_specs=pl.BlockSpec((tm, D), lambda i: (i, 0)),
        compiler_params=_params(("parallel",)),
        name="ln_in",
    )(x, g.reshape(1, D), b.reshape(1, D))


C_FK = 0
C_MQK = C_FK + FOX_WIDTH
C_MV = C_MQK + 2 * MLSTM_WIDTH
C_MO = C_MV + MLSTM_WIDTH
C_SU = C_MO + MLSTM_WIDTH
C_SV = C_SU + SGU_WIDTH
C_GATE = C_SV + SGU_WIDTH
C_END = C_GATE + N_BRANCH * D_MODEL
PROJ_PIECE = 512


def _inproj_kernel(x_ref, wm_ref, bm_ref, wqvt_ref, bqvt_ref, ws_ref, bs_ref, wst_ref, bst_ref,
                   sg_ref, sb_ref, swt_ref, sbias_ref,
                   qt_ref, fk_ref, vt_ref, mqk_ref, mv_ref, mo_ref, yc_ref, gt_ref, sm_ref, smt_ref,
                   wbd_scr):
    tm = x_ref.shape[0]

    @pl.when(pl.program_id(0) == 0)
    def _():
        r = lax.broadcasted_iota(jnp.int32, (SGU_BLOCK, SGU_BLOCK), 0)
        c = lax.broadcasted_iota(jnp.int32, (SGU_BLOCK, SGU_BLOCK), 1)
        keep = jnp.logical_and(r // SGU_CHUNK == c // SGU_CHUNK, r >= c)
        for g in range(SGU_GROUPS):
            wbd_scr[g] = jnp.where(keep, swt_ref[g], 0.0).astype(BF16)

    xb = x_ref[...].astype(BF16)

    def proj(lo, width=PROJ_PIECE):
        return _dot(xb, wm_ref[:, lo:lo + width]) + bm_ref[:, lo:lo + width]

    fk_ref[...] = proj(C_FK).astype(BF16)
    qvt = _dot_nt(wqvt_ref[...], xb) + bqvt_ref[...]
    qt_ref[...] = (qvt[:FOX_WIDTH] * (FOX_HEAD_DIM ** -0.5)).astype(BF16)
    vt_ref[...] = qvt[FOX_WIDTH:].astype(BF16)
    for j in range(2):
        mqk_ref[:, j * PROJ_PIECE:(j + 1) * PROJ_PIECE] = proj(C_MQK + j * PROJ_PIECE).astype(BF16)
    mv_ref[...] = proj(C_MV).astype(BF16)
    mo_ref[...] = jax.nn.sigmoid(proj(C_MO)).astype(BF16)
    for j in range(N_BRANCH * D_MODEL // PROJ_PIECE):
        gt_ref[:, j * PROJ_PIECE:(j + 1) * PROJ_PIECE] = jax.nn.sigmoid(
            proj(C_GATE + j * PROJ_PIECE)).astype(BF16)

    sm = _dot(xb, ws_ref[...]) + bs_ref[...]
    lane = lax.broadcasted_iota(jnp.int32, sm.shape, 1)
    is_ls = jnp.logical_or(lane < SM_MI, jnp.logical_and(lane >= SM_MF, lane < SM_ROWS))
    sm_ref[...] = jnp.where(is_ls, _log_sigmoid(sm), sm)
    smt = _dot_nt(wst_ref[...], xb) + bst_ref[...]
    row = lax.broadcasted_iota(jnp.int32, smt.shape, 0)
    is_ls_t = jnp.logical_or(row < SM_MI, row >= SM_MF)
    smt_ref[...] = jnp.where(is_ls_t, _log_sigmoid(smt), smt)

    u = _gelu_tanh(proj(C_SU))
    v = _gelu_tanh(proj(C_SV))
    vn = _layer_norm(v, sg_ref[...], sb_ref[...]).astype(BF16)
    for rb in range(tm // SGU_BLOCK):
        rows = slice(rb * SGU_BLOCK, (rb + 1) * SGU_BLOCK)
        for g in range(SGU_GROUPS):
            cols = slice(g * LANES, (g + 1) * LANES)
            z = _dot(wbd_scr[g], vn[rows, cols]) + sbias_ref[rows, cols]
            yc_ref[rows, cols] = (u[rows, cols] * z).astype(BF16)


def _in_projection(x, p, tm):
    T, D = x.shape
    row = lambda w: pl.BlockSpec((tm, w), lambda i: (i, 0))
    out_shapes = (
        jax.ShapeDtypeStruct((FOX_WIDTH, T), BF16),
        jax.ShapeDtypeStruct((T, FOX_WIDTH), BF16),
        jax.ShapeDtypeStruct((FOX_WIDTH, T), BF16),
        jax.ShapeDtypeStruct((T, 2 * MLSTM_WIDTH), BF16),
        jax.ShapeDtypeStruct((T, MLSTM_WIDTH), BF16),
        jax.ShapeDtypeStruct((T, MLSTM_WIDTH), BF16),
        jax.ShapeDtypeStruct((T, SGU_WIDTH), BF16),
        jax.ShapeDtypeStruct((T, N_BRANCH * D_MODEL), BF16),
        jax.ShapeDtypeStruct((T, LANES), F32),
        jax.ShapeDtypeStruct((SM_ROWS, T), F32),
    )
    col = lambda h: pl.BlockSpec((h, tm), lambda i: (0, i))
    out_specs = (col(FOX_WIDTH), row(FOX_WIDTH), col(FOX_WIDTH),
                 row(2 * MLSTM_WIDTH), row(MLSTM_WIDTH), row(MLSTM_WIDTH), row(SGU_WIDTH),
                 row(N_BRANCH * D_MODEL), row(LANES), col(SM_ROWS))
    const = lambda shape: _resident(shape, lambda i: tuple(0 for _ in shape))
    in_specs = [
        row(D),
        const((D, C_END)), const((1, C_END)),
        const((2 * FOX_WIDTH, D)), const((2 * FOX_WIDTH, 1)),
        const((D, LANES)), const((1, LANES)),
        const((SM_ROWS, D)), const((SM_ROWS, 1)),
        const((1, SGU_WIDTH)), const((1, SGU_WIDTH)),
        const((SGU_GROUPS, SGU_BLOCK, SGU_BLOCK)),
        const((tm, SGU_WIDTH)),
    ]
    return pl.pallas_call(
        _inproj_kernel,
        out_shape=out_shapes,
        grid=(T // tm,),
        in_specs=in_specs,
        out_specs=out_specs,
        scratch_shapes=[pltpu.VMEM((SGU_GROUPS, SGU_BLOCK, SGU_BLOCK), BF16)],
        compiler_params=_params(("arbitrary",)),
        name="in_proj",
    )(x, p["wm"], p["bm"], p["wqvt"], p["bqvt"], p["ws"], p["bs"], p["wst"], p["bst"],
      p["sgu_g"], p["sgu_b"], p["sgu_wt"], p["sgu_bias"])


def _fcum_kernel(sm_ref, smt_ref, fcol_ref, frow_ref):
    S = sm_ref.shape[0]
    cb = min(CUMSUM_BLOCK, S)
    r = lax.broadcasted_iota(jnp.int32, (cb, cb), 0)
    c = lax.broadcasted_iota(jnp.int32, (cb, cb), 1)
    lower = (r >= c).astype(F32)
    upper = (r <= c).astype(F32)
    carry_c = jnp.zeros((1, LANES), F32)
    carry_r = jnp.zeros((SM_ROWS, 1), F32)
    for blk in range(S // cb):
        sl = slice(blk * cb, (blk + 1) * cb)
        fc = _dot_f32(lower, sm_ref[sl, :]) + carry_c
        fcol_ref[sl, :] = fc
        carry_c = fc[cb - 1:cb, :]
        fr = _dot_f32(smt_ref[:, sl], upper) + carry_r
        frow_ref[:, sl] = fr
        carry_r = fr[:, cb - 1:cb]


def _forget_cumsum(sm, smt, B, S):
    T = B * S
    return pl.pallas_call(
        _fcum_kernel,
        out_shape=(jax.ShapeDtypeStruct((T, LANES), F32), jax.ShapeDtypeStruct((SM_ROWS, T), F32)),
        grid=(B,),
        in_specs=[pl.BlockSpec((S, LANES), lambda b: (b, 0)),
                  pl.BlockSpec((SM_ROWS, S), lambda b: (0, b))],
        out_specs=(pl.BlockSpec((S, LANES), lambda b: (b, 0)),
                   pl.BlockSpec((SM_ROWS, S), lambda b: (0, b))),
        compiler_params=_params(("parallel",)),
        name="forget_cumsum",
    )(sm, smt)


N_SPLIT = 3


def _split3(f):
    hi = f.astype(BF16).astype(F32)
    r1 = f - hi
    mid = r1.astype(BF16).astype(F32)
    lo = r1 - mid
    return hi, mid, lo


def _fox_kernel(qt_ref, k_ref, vt_ref, fcol_ref, frow_ref, o_ref, kaug_scr, *, tq):
    S = k_ref.shape[0]
    pair = pl.program_id(1)
    qi = pl.program_id(2)

    @pl.when(qi == 0)
    def _():
        lane = lax.broadcasted_iota(jnp.int32, (S, LANES), 1)
        fc = fcol_ref[...]
        k = k_ref[...]
        for hh in range(2):
            f = jnp.sum(jnp.where(lane == 2 * pair + hh, fc, 0.0), axis=1, keepdims=True)
            hi, mid, lo = _split3(f)
            aug = jnp.where(lane < N_SPLIT, 1.0,
                            jnp.where(lane == N_SPLIT, -hi,
                                      jnp.where(lane == N_SPLIT + 1, -mid,
                                                jnp.where(lane == N_SPLIT + 2, -lo, 0.0))))
            kaug_scr[hh, :, 0:LANES] = k
            kaug_scr[hh, :, LANES:2 * LANES] = aug.astype(BF16)

    qt = qt_ref[...]
    sub = lax.broadcasted_iota(jnp.int32, (LANES, tq), 0)
    key = lax.broadcasted_iota(jnp.int32, (tq, tq), 0)
    qry = lax.broadcasted_iota(jnp.int32, (tq, tq), 1)
    causal = key <= qry

    rhs = []
    for hh in range(2):
        hi, mid, lo = _split3(frow_ref[hh])
        augq = jnp.where(sub == 0, hi,
                         jnp.where(sub == 1, mid,
                                   jnp.where(sub == 2, lo,
                                             jnp.where(sub < 2 * N_SPLIT, 1.0, 0.0))))
        in_head = (sub < FOX_HEAD_DIM) if hh == 0 else (sub >= FOX_HEAD_DIM)
        qm = jnp.where(in_head, qt, jnp.zeros_like(qt))
        rhs.append(jnp.concatenate([qm, augq.astype(BF16)], axis=0))

    def block(j, carry, masked):
        off = pl.multiple_of(j * tq, tq)
        vt_blk = vt_ref[:, pl.ds(off, tq)]
        new = []
        scores = [_dot(kaug_scr[hh, pl.ds(off, tq), :], rhs[hh]) for hh in range(2)]
        for hh in range(2):
            m, l, acc = carry[hh]
            s = scores[hh]
            if masked:
                s = jnp.where(causal, s, NEG_INF)
            m_new = jnp.maximum(m, jnp.max(s, axis=0, keepdims=True))
            p = jnp.exp(s - m_new)
            corr = jnp.exp(m - m_new)
            l = l * corr + jnp.sum(p, axis=0, keepdims=True)
            acc = acc * corr + _dot(vt_blk, p.astype(BF16))
            new.append((m_new, l, acc))
        return tuple(new)

    init_head = (jnp.full((1, tq), NEG_INF, F32), jnp.zeros((1, tq), F32), jnp.zeros((LANES, tq), F32))
    carry = lax.fori_loop(0, qi, functools.partial(block, masked=False), (init_head, init_head))
    carry = block(qi, carry, True)
    outs = [acc / l for (_, l, acc) in carry]
    out_t = jnp.where(sub < FOX_HEAD_DIM, outs[0], outs[1])
    o_ref[...] = out_t.T.astype(o_ref.dtype)


def _fox_attention(qt, fk, vt, fcol, frow, B, S, tq):
    T = B * S
    nq = S // tq
    n_pairs = FOX_HEADS // 2
    frow3 = frow.reshape(SM_ROWS, 1, T)
    return pl.pallas_call(
        functools.partial(_fox_kernel, tq=tq),
        out_shape=jax.ShapeDtypeStruct((T, FOX_WIDTH), BF16),
        grid=(B, n_pairs, nq),
        in_specs=[pl.BlockSpec((LANES, tq), lambda b, p, i: (p, b * nq + i)),
                  pl.BlockSpec((S, LANES), lambda b, p, i: (b, p)),
                  pl.BlockSpec((LANES, S), lambda b, p, i: (p, b)),
                  pl.BlockSpec((S, LANES), lambda b, p, i: (b, 0)),
                  pl.BlockSpec((2, 1, tq), lambda b, p, i: (p, 0, b * nq + i))],
        out_specs=pl.BlockSpec((tq, LANES), lambda b, p, i: (b * nq + i, p)),
        scratch_shapes=[pltpu.VMEM((2, S, 2 * LANES), BF16)],
        compiler_params=_params(("parallel", "parallel", "arbitrary")),
        name="fox_attention",
    )(qt, fk, vt, fcol, frow3)


def _mlstm_kernel(qk_ref, v_ref, og_ref, sm_ref, smt_ref, cw_ref, y_ref, c_scr, n_scr, m_scr, tail_scr):
    L = MLSTM_CHUNK
    dh = MLSTM_HEAD_DIM

    @pl.when(pl.program_id(1) == 0)
    def _():
        c_scr[...] = jnp.zeros_like(c_scr)
        n_scr[...] = jnp.zeros_like(n_scr)
        m_scr[...] = jnp.zeros_like(m_scr)
        tail_scr[...] = jnp.zeros_like(tail_scr)

    x = qk_ref[...].astype(F32)
    xe = jnp.concatenate([tail_scr[...], x], axis=0)
    cw = cw_ref[...]
    y = x * cw[MLSTM_CONV - 1:MLSTM_CONV, :]
    for k in range(1, MLSTM_CONV):
        shifted = pltpu.roll(xe, k, 0)[SUBLANES:, :]
        y = y + shifted * cw[MLSTM_CONV - 1 - k:MLSTM_CONV - k, :]
    tail_scr[...] = x[L - SUBLANES:, :]
    y = y * jax.nn.sigmoid(y)
    q_all = y[:, :MLSTM_WIDTH]
    k_all = y[:, MLSTM_WIDTH:] * (dh ** -0.5)

    sm = sm_ref[...]
    smt = smt_ref[...]
    r = lax.broadcasted_iota(jnp.int32, (L, L), 0)
    c = lax.broadcasted_iota(jnp.int32, (L, L), 1)
    causal = r >= c
    bcol_all = _dot_f32(causal.astype(F32), sm)
    brow_all = _dot_f32(smt, (r <= c).astype(F32))

    for h in range(MLSTM_HEADS):
        hs = slice(h * dh, (h + 1) * dh)
        bq = bcol_all[:, SM_MF + h:SM_MF + h + 1]
        li_c = sm[:, SM_MI + h:SM_MI + h + 1]
        br = brow_all[SM_MF + h:SM_MF + h + 1, :]
        li_r = smt[SM_MI + h:SM_MI + h + 1, :]
        m_prev = m_scr[h:h + 1, 0:1]
        b_last = bq[L - 1:L, :]

        d = jnp.where(causal, bq - br + li_r, NEG_INF)
        inter = bq + m_prev
        m_t = jnp.maximum(inter, jnp.max(d, axis=1, keepdims=True))
        w_intra = jnp.exp(d - m_t)
        w_inter = jnp.exp(inter - m_t)

        qh = q_all[:, hs]
        qb = qh.astype(BF16)
        kh = k_all[:, hs]
        vh = v_ref[:, hs]
        qk = _dot_nt(qb, kh.astype(BF16)) * w_intra
        c_prev = c_scr[h]
        n_prev = n_scr[h:h + 1, :]
        num = _dot(qk.astype(BF16), vh) + w_inter * _dot(qb, c_prev.astype(BF16))
        den = jnp.sum(qk, axis=1, keepdims=True) + w_inter * jnp.sum(qh * n_prev, axis=1, keepdims=True)
        h_c = num / jnp.maximum(jnp.abs(den), jnp.exp(-m_t))
        y_ref[:, hs] = (og_ref[:, hs].astype(F32) * h_c).astype(y_ref.dtype)

        g_c = b_last - bq + li_c
        m_new = jnp.maximum(b_last + m_prev, jnp.max(g_c, axis=0, keepdims=True))
        decay = jnp.exp(b_last + m_prev - m_new)
        kw = kh * jnp.exp(g_c - m_new)
        c_scr[h] = decay * c_prev + _dot_tn(kw.astype(BF16), vh)
        n_scr[h:h + 1, :] = decay * n_prev + jnp.sum(kw, axis=0, keepdims=True)
        m_scr[h:h + 1, :] = jnp.broadcast_to(m_new, (1, LANES))


def _mlstm(mqk, mv, mo, sm, smt, conv_w, B, S):
    T = B * S
    L = MLSTM_CHUNK
    nc = S // L
    row = lambda w: pl.BlockSpec((L, w), lambda b, c: (b * nc + c, 0))
    return pl.pallas_call(
        _mlstm_kernel,
        out_shape=jax.ShapeDtypeStruct((T, MLSTM_WIDTH), BF16),
        grid=(B, nc),
        in_specs=[row(2 * MLSTM_WIDTH), row(MLSTM_WIDTH), row(MLSTM_WIDTH), row(LANES),
                  pl.BlockSpec((SM_ROWS, L), lambda b, c: (0, b * nc + c)),
                  pl.BlockSpec((MLSTM_CONV, 2 * MLSTM_WIDTH), lambda b, c: (0, 0))],
        out_specs=row(MLSTM_WIDTH),
        scratch_shapes=[pltpu.VMEM((MLSTM_HEADS, MLSTM_HEAD_DIM, MLSTM_HEAD_DIM), F32),
                        pltpu.VMEM((SUBLANES, MLSTM_HEAD_DIM), F32),
                        pltpu.VMEM((SUBLANES, LANES), F32),
                        pltpu.VMEM((SUBLANES, 2 * MLSTM_WIDTH), F32)],
        compiler_params=_params(("parallel", "arbitrary")),
        name="mlstm",
    )(mqk, mv, mo, sm, smt, conv_w)


def _merge_kernel(ya_ref, yb_ref, yc_ref, gt_ref, x_ref, wb_ref, wo_ref, g_ref, b_ref, wr_ref, br_ref,
                  x1_ref, rt_ref, cnt_ref, carry_scr):
    tm = x_ref.shape[0]
    i = pl.program_id(0)

    @pl.when(i == 0)
    def _():
        carry_scr[...] = jnp.zeros_like(carry_scr)

    merged = None
    for n, y_ref in enumerate((ya_ref, yb_ref, yc_ref)):
        pr = _dot(y_ref[...], wb_ref[n]) * gt_ref[:, n * D_MODEL:(n + 1) * D_MODEL].astype(F32)
        merged = pr if merged is None else merged + pr
    mix = _dot(merged.astype(BF16), wo_ref[...])
    x1 = _layer_norm(DEEPNORM_ALPHA * x_ref[...] + mix, g_ref[...], b_ref[...])
    x1_ref[...] = x1

    logits = _dot(x1.astype(BF16), wr_ref[...]) + br_ref[...]
    lane = lax.broadcasted_iota(jnp.int32, (tm, LANES), 1)
    big = jnp.int32(LANES)
    glog = jnp.where(lane < N_GROUPS, logits, -jnp.inf)
    gmax = jnp.max(glog, axis=1, keepdims=True)
    g_top = jnp.min(jnp.where(glog == gmax, lane, big), axis=1, keepdims=True)
    p_g = 1.0 / jnp.sum(jnp.exp(glog - gmax), axis=1, keepdims=True)
    lo = ROUTER_EXPERT_LANE0 + EXPERTS_PER_GROUP * g_top
    el = jnp.where(jnp.logical_and(lane >= lo, lane < lo + EXPERTS_PER_GROUP), logits, -jnp.inf)
    m1 = jnp.max(el, axis=1, keepdims=True)
    i1 = jnp.min(jnp.where(el == m1, lane, big), axis=1, keepdims=True)
    el2 = jnp.where(lane == i1, -jnp.inf, el)
    m2 = jnp.max(el2, axis=1, keepdims=True)
    i2 = jnp.min(jnp.where(el2 == m2, lane, big), axis=1, keepdims=True)
    ratio = jnp.exp(m2 - m1)
    gate1 = p_g / (1.0 + ratio)
    gate2 = p_g * ratio / (1.0 + ratio)

    hit1 = lane == i1
    hit2 = lane == i2
    onehot = jnp.where(jnp.logical_or(hit1, hit2), 1.0, 0.0)
    r = lax.broadcasted_iota(jnp.int32, (tm, tm), 0)
    c = lax.broadcasted_iota(jnp.int32, (tm, tm), 1)
    before = jnp.where(r > c, 1.0, 0.0).astype(BF16)
    seen = _dot(before, onehot.astype(BF16)) + carry_scr[0:1, :]
    rank1 = jnp.sum(jnp.where(hit1, seen, 0.0), axis=1, keepdims=True)
    rank2 = jnp.sum(jnp.where(hit2, seen, 0.0), axis=1, keepdims=True)
    total = carry_scr[0:1, :] + jnp.sum(onehot, axis=0, keepdims=True)
    carry_scr[...] = jnp.broadcast_to(total, carry_scr.shape)
    cnt_ref[...] = jnp.broadcast_to(total, cnt_ref.shape)

    e1 = (i1 - ROUTER_EXPERT_LANE0).astype(F32)
    e2 = (i2 - ROUTER_EXPERT_LANE0).astype(F32)
    rec = jnp.zeros((tm, LANES), F32)
    for pos, val in ((RT_E1, e1), (RT_E2, e2), (RT_G1, gate1), (RT_G2, gate2), (RT_R1, rank1), (RT_R2, rank2)):
        rec = jnp.where(lane == pos, val, rec)
    rt_ref[...] = rec[:, :RT_WIDTH]


def _merge_project_route(ya, yb, yc, gates, x, p, tm):
    T, D = x.shape
    row = lambda w: pl.BlockSpec((tm, w), lambda i: (i, 0))
    const = lambda shape: _resident(shape, lambda i: tuple(0 for _ in shape))
    return pl.pallas_call(
        _merge_kernel,
        out_shape=(jax.ShapeDtypeStruct((T, D), F32),
                   jax.ShapeDtypeStruct((T, RT_WIDTH), F32),
                   jax.ShapeDtypeStruct((SUBLANES, LANES), F32)),
        grid=(T // tm,),
        in_specs=[row(FOX_WIDTH), row(MLSTM_WIDTH), row(SGU_WIDTH), row(N_BRANCH * D), row(D),
                  const((N_BRANCH, FOX_WIDTH, D)), const((D, D)), const((1, D)), const((1, D)),
                  const((D, LANES)), const((1, LANES))],
        out_specs=(row(D), row(RT_WIDTH), pl.BlockSpec((SUBLANES, LANES), lambda i: (0, 0))),
        scratch_shapes=[pltpu.VMEM((SUBLANES, LANES), F32)],
        compiler_params=_params(("arbitrary",)),
        name="merge_route",
    )(ya, yb, yc, gates, x, p["wb"], p["wo"], p["ln1_g"], p["ln1_b"], p["wr"], p["br"])


def _expert_kernel(be_ref, nu_ref, src0_ref, src_next_ref, dst_ref, x_hbm, wg_ref, wu_ref, wd_ref, ys_hbm,
                   xbuf, ybuf, wg_scr, wu_scr, wd_scr, gsem, ssem):
    R = MOE_ROWS
    i = pl.program_id(0)
    nb = pl.num_programs(0)
    n_used = nu_ref[0]
    slot = lax.rem(i, 2)
    active = i < n_used

    def gather(idx_ref, s):
        def issue(r, carry):
            pltpu.make_async_copy(x_hbm.at[pl.ds(idx_ref[0, 0, r], 1)], xbuf.at[s, pl.ds(r, 1)], gsem.at[s]).start()
            return carry
        lax.fori_loop(0, R, issue, 0, unroll=8)

    def gather_wait(s):
        pltpu.make_async_copy(x_hbm.at[pl.ds(0, R)], xbuf.at[s], gsem.at[s]).wait()

    def scatter(s):
        def issue(r, carry):
            pltpu.make_async_copy(ybuf.at[s, pl.ds(r, 1)], ys_hbm.at[pl.ds(dst_ref[0, 0, r], 1)], ssem.at[s]).start()
            return carry
        lax.fori_loop(0, R, issue, 0, unroll=8)

    def scatter_wait(s):
        pltpu.make_async_copy(ybuf.at[s], ys_hbm.at[pl.ds(0, R)], ssem.at[s]).wait()

    @pl.when(i == 0)
    def _():
        ybuf[1] = jnp.zeros((R, D_MODEL), F32)
        n_token_rows = ys_hbm.shape[0] - 2 * R
        for part in range(2):
            fill = pltpu.make_async_copy(ybuf.at[1], ys_hbm.at[pl.ds(n_token_rows + part * R, R)], ssem.at[1])
            fill.start()
            fill.wait()

    @pl.when(jnp.logical_and(i == 0, active))
    def _():
        gather(src0_ref, 0)

    @pl.when(i + 1 < n_used)
    def _():
        gather(src_next_ref, 1 - slot)

    new_expert = jnp.logical_or(i == 0, be_ref[i] != be_ref[jnp.maximum(i - 1, 0)])

    @pl.when(jnp.logical_and(active, new_expert))
    def _():
        wg_scr[...] = wg_ref[...].astype(BF16)
        wu_scr[...] = wu_ref[...].astype(BF16)
        wd_scr[...] = wd_ref[...].astype(BF16)

    @pl.when(active)
    def _():
        gather_wait(slot)
        xb = xbuf[slot].astype(BF16)
        gate = _dot(xb, wg_scr[...])
        up = _dot(xb, wu_scr[...])
        hid = (gate * jax.nn.sigmoid(gate) * up).astype(BF16)
        ybuf[slot] = _dot(hid, wd_scr[...])
        scatter(slot)

    @pl.when(jnp.logical_and(i >= 1, i - 1 < n_used))
    def _():
        scatter_wait(1 - slot)

    @pl.when(jnp.logical_and(i == nb - 1, active))
    def _():
        scatter_wait(slot)


def _expert_mlp(x1, src_tok, dst_row, block_expert, n_used, w_gate, w_up, w_down, n_out_rows):
    T, D = x1.shape
    nb = src_tok.shape[0]
    R = MOE_ROWS
    wsel = lambda i, be, nu: (be[i], 0, 0)
    idx_spec = lambda f: pl.BlockSpec((1, 1, R), f, memory_space=pltpu.SMEM)
    return pl.pallas_call(
        _expert_kernel,
        out_shape=jax.ShapeDtypeStruct((n_out_rows, D), F32),
        grid_spec=pltpu.PrefetchScalarGridSpec(
            num_scalar_prefetch=2,
            grid=(nb,),
            in_specs=[idx_spec(lambda i, be, nu: (0, 0, 0)),
                      idx_spec(lambda i, be, nu: (jnp.minimum(i + 1, nb - 1), 0, 0)),
                      idx_spec(lambda i, be, nu: (i, 0, 0)),
                      pl.BlockSpec(memory_space=pl.ANY),
                      pl.BlockSpec((None, D_MODEL, D_EXPERT), wsel),
                      pl.BlockSpec((None, D_MODEL, D_EXPERT), wsel),
                      pl.BlockSpec((None, D_EXPERT, D_MODEL), wsel)],
            out_specs=pl.BlockSpec(memory_space=pl.ANY),
            scratch_shapes=[pltpu.VMEM((2, R, D), F32),
                            pltpu.VMEM((2, R, D), F32),
                            pltpu.VMEM((D_MODEL, D_EXPERT), BF16),
                            pltpu.VMEM((D_MODEL, D_EXPERT), BF16),
                            pltpu.VMEM((D_EXPERT, D_MODEL), BF16),
                            pltpu.SemaphoreType.DMA((2,)),
                            pltpu.SemaphoreType.DMA((2,))]),
        compiler_params=_params(("arbitrary",)),
        name="moe_experts",
    )(block_expert, n_used, src_tok, src_tok, dst_row, x1, w_gate, w_up, w_down)


def _combine_kernel(y1_ref, y2_ref, rt_ref, x1_ref, g_ref, b_ref, o_ref):
    rt = rt_ref[...]
    ffn = rt[:, RT_G1:RT_G1 + 1] * y1_ref[...] + rt[:, RT_G2:RT_G2 + 1] * y2_ref[...]
    o_ref[...] = _layer_norm(DEEPNORM_ALPHA * x1_ref[...] + ffn, g_ref[...], b_ref[...])


def _combine(ys, route, x1, g, b, th):
    T, D = x1.shape
    nt = T // th
    return pl.pallas_call(
        _combine_kernel,
        out_shape=jax.ShapeDtypeStruct((T, D), F32),
        grid=(nt,),
        in_specs=[pl.BlockSpec((th, D), lambda i: (i, 0)),
                  pl.BlockSpec((th, D), lambda i: (nt + i, 0)),
                  pl.BlockSpec((th, RT_WIDTH), lambda i: (i, 0)),
                  pl.BlockSpec((th, D), lambda i: (i, 0)),
                  pl.BlockSpec((1, D), lambda i: (0, 0)),
                  pl.BlockSpec((1, D), lambda i: (0, 0))],
        out_specs=pl.BlockSpec((th, D), lambda i: (i, 0)),
        compiler_params=_params(("parallel",)),
        name="moe_combine",
    )(ys, ys, route, x1, g, b)


def _moe(x1, route, counts, p, th):
    T = x1.shape[0]
    R = MOE_ROWS
    A = T * TOP_K
    cnt = counts[0, ROUTER_EXPERT_LANE0:ROUTER_EXPERT_LANE0 + N_EXPERTS].astype(jnp.int32)
    padded = (cnt + R - 1) // R * R
    pend = jnp.cumsum(padded)
    pstart = pend - padded
    expert = route[:, RT_E1:RT_E2 + 1].astype(jnp.int32)
    rank = route[:, RT_R1:RT_R2 + 1].astype(jnp.int32)
    slot = (pstart[expert] + rank).reshape(-1)
    nb = -(-A // R) + N_EXPERTS
    n_rows = nb * R
    inv = jnp.full((n_rows,), -1, jnp.int32).at[slot].set(
        jnp.arange(A, dtype=jnp.int32), unique_indices=True, mode="promise_in_bounds")
    valid = inv >= 0
    tok = jnp.where(valid, inv >> 1, 0)
    row = jnp.arange(n_rows, dtype=jnp.int32)
    spill = A + ((row // R) % 2) * R + row % R
    dst = jnp.where(valid, (inv & 1) * T + tok, spill)
    block_start = jnp.arange(nb, dtype=jnp.int32) * R
    block_expert = jnp.minimum(
        jnp.sum((pend[None, :] <= block_start[:, None]).astype(jnp.int32), axis=1), N_EXPERTS - 1)
    n_used = (pend[-1:] // R).astype(jnp.int32)
    ys = _expert_mlp(x1, tok.reshape(nb, 1, R), dst.reshape(nb, 1, R), block_expert, n_used,
                     p["w_gate"], p["w_up"], p["w_down"], A + 2 * R)
    return _combine(ys, route, x1, p["ln2_g"], p["ln2_b"], th)


def _layer_params(l, w_in, b_in, conv_w, sgu_ln_g, sgu_ln_b, sgu_w, sgu_b, w_branch, w_out, ln1_g, ln1_b,
                  w_group, b_group, w_router, b_router, w_gate, w_up, w_down, ln2_g, ln2_b, tm):
    offs = [0]
    for s in IN_SIZES:
        offs.append(offs[-1] + s)
    wt, b = w_in[l].T, b_in[l]
    seg = lambda a, i: a[offs[i]:offs[i + 1]]
    fox_wt, fox_b = seg(wt, 0), seg(b, 0)
    fq_wt, fk_wt, fv_wt = (fox_wt[j * FOX_WIDTH:(j + 1) * FOX_WIDTH] for j in range(3))
    fq_b, fk_b, fv_b = (fox_b[j * FOX_WIDTH:(j + 1) * FOX_WIDTH] for j in range(3))
    wmt = jnp.concatenate([fk_wt, seg(wt, 2), seg(wt, 3), seg(wt, 4), seg(wt, 6), seg(wt, 7)], axis=0)
    bm = jnp.concatenate([fk_b, seg(b, 2), seg(b, 3), seg(b, 4), seg(b, 6), seg(b, 7)])
    n_small = IN_SIZES[1] + IN_SIZES[5]
    wst = jnp.concatenate([seg(wt, 1), seg(wt, 5)], axis=0).astype(BF16)
    ws = jnp.concatenate([wst, jnp.zeros((LANES - n_small, D_MODEL), BF16)], axis=0).T
    bs = jnp.concatenate([seg(b, 1), seg(b, 5), jnp.zeros((LANES - n_small,), F32)])
    reps = SGU_BLOCK // SGU_CHUNK
    sgu_bias = jnp.broadcast_to(sgu_b[l].T[:, :, None], (SGU_CHUNK, SGU_GROUPS, SGU_WIDTH // SGU_GROUPS))
    sgu_bias = jnp.tile(sgu_bias.reshape(SGU_CHUNK, SGU_WIDTH), (tm // SGU_CHUNK, 1))
    n_route = N_GROUPS + N_EXPERTS
    wr = jnp.concatenate([w_group[l], w_router[l], jnp.zeros((D_MODEL, LANES - n_route), F32)], axis=1)
    br = jnp.concatenate([b_group[l], b_router[l], jnp.zeros((LANES - n_route,), F32)])
    return {
        "wm": wmt.astype(BF16).T, "bm": bm.reshape(1, -1),
        "wqvt": jnp.concatenate([fq_wt, fv_wt], axis=0).astype(BF16),
        "bqvt": jnp.concatenate([fq_b, fv_b]).reshape(-1, 1),
        "ws": ws, "bs": bs.reshape(1, -1),
        "wst": wst, "bst": bs[:SM_ROWS].reshape(-1, 1),
        "conv_w": conv_w[l],
        "sgu_g": sgu_ln_g[l].reshape(1, -1), "sgu_b": sgu_ln_b[l].reshape(1, -1),
        "sgu_wt": jnp.tile(sgu_w[l], (1, reps, reps)), "sgu_bias": sgu_bias,
        "wb": w_branch[l].astype(BF16), "wo": w_out[l].astype(BF16),
        "ln1_g": ln1_g[l].reshape(1, -1), "ln1_b": ln1_b[l].reshape(1, -1),
        "wr": wr.astype(BF16), "br": br.reshape(1, -1),
        "w_gate": w_gate[l], "w_up": w_up[l], "w_down": w_down[l],
        "ln2_g": ln2_g[l].reshape(1, -1), "ln2_b": ln2_b[l].reshape(1, -1),
    }


def _tiles(B, S):
    T = B * S
    tm = 512 if T % 512 == 0 else 256
    tq = 512 if S % 512 == 0 else 256
    return tm, tq


def kernel(x, ln_in_g, ln_in_b, w_in, b_in, conv_w, sgu_ln_g, sgu_ln_b, sgu_w, sgu_b, w_branch, w_out,
           ln1_g, ln1_b, w_group, b_group, w_router, b_router, w_gate, w_up, w_down, ln2_g, ln2_b):
    B, S, D = x.shape
    assert D == D_MODEL and S % SGU_BLOCK == 0
    T = B * S
    tm, tq = _tiles(B, S)
    h = _input_layer_norm(x.reshape(T, D), ln_in_g, ln_in_b, tm)
    for l in range(DEPTH):
        p = _layer_params(l, w_in, b_in, conv_w, sgu_ln_g, sgu_ln_b, sgu_w, sgu_b, w_branch, w_out, ln1_g,
                          ln1_b, w_group, b_group, w_router, b_router, w_gate, w_up, w_down, ln2_g, ln2_b, tm)
        qt, fk, vt, mqk, mv, mo, yc, gates, sm, smt = _in_projection(h, p, tm)
        fcol, frow = _forget_cumsum(sm, smt, B, S)
        ya = _fox_attention(qt, fk, vt, fcol, frow, B, S, tq)
        yb = _mlstm(mqk, mv, mo, sm, smt, p["conv_w"], B, S)
        x1, route, counts = _merge_project_route(ya, yb, yc, gates, h, p, tm)
        h = _moe(x1, route, counts, p, tm)
    return h.reshape(B, S, D)
```

```python
import functools

import jax
import jax.numpy as jnp
from jax import lax
from jax.experimental import pallas as pl
from jax.experimental.pallas import tpu as pltpu

F32 = jnp.float32
BF16 = jnp.bfloat16

D_MODEL = 1024
DEPTH = 2
FOX_HEADS = 8
FOX_HEAD_DIM = 64
FOX_WIDTH = FOX_HEADS * FOX_HEAD_DIM
MLSTM_HEADS = 4
MLSTM_HEAD_DIM = 128
MLSTM_WIDTH = MLSTM_HEADS * MLSTM_HEAD_DIM
MLSTM_CHUNK = 128
MLSTM_CONV = 4
SGU_GROUPS = 4
SGU_WIDTH = 512
SGU_CHUNK = 128
N_BRANCH = 3
N_GROUPS = 4
EXPERTS_PER_GROUP = 8
N_EXPERTS = N_GROUPS * EXPERTS_PER_GROUP
TOP_K = 2
D_EXPERT = 512
DEEPNORM_ALPHA = (2 * DEPTH) ** 0.25
LN_EPS = 1e-5
NEG_INF = -1e30

IN_SIZES = (3 * FOX_WIDTH, FOX_HEADS, 2 * MLSTM_WIDTH, MLSTM_WIDTH, MLSTM_WIDTH,
            2 * MLSTM_HEADS, 2 * SGU_WIDTH, N_BRANCH * D_MODEL)

LANES = 128
SUBLANES = 8
VMEM_LIMIT_BYTES = 56 * 1024 * 1024

SM_FOXF = 0
SM_MI = 8
SM_MF = 12
SM_ROWS = 16

RT_E1, RT_E2, RT_P1, RT_P2 = 0, 1, 2, 3
RT_WIDTH = 8
ROUTER_EXPERT_LANE0 = N_GROUPS

MOE_ROWS = 256
MOE_ALIGN = 16
MOE_CHUNKS = MOE_ROWS // MOE_ALIGN
XS_G1, XS_G2, XS_E1 = 0, 3, 6
XS_WIDTH = D_MODEL + LANES
SGU_BLOCK = 256
CUMSUM_BLOCK = 256


def _params(sem, vmem=VMEM_LIMIT_BYTES):
    return pltpu.CompilerParams(dimension_semantics=sem, vmem_limit_bytes=vmem)


def _resident(shape, index_map):
    return pl.BlockSpec(shape, index_map, pipeline_mode=pl.Buffered(1))


def _layer_norm(x, g, b):
    mu = jnp.mean(x, axis=-1, keepdims=True)
    xc = x - mu
    var = jnp.mean(xc * xc, axis=-1, keepdims=True)
    return xc * lax.rsqrt(var + LN_EPS) * g + b


def _log_sigmoid(x):
    return jnp.minimum(x, 0.0) - jnp.log1p(jnp.exp(-jnp.abs(x)))


def _gelu_tanh(x):
    return 0.5 * x * (1.0 + jnp.tanh(0.7978845608028654 * (x + 0.044715 * x * x * x)))


def _dot(a, b):
    return jnp.dot(a, b, preferred_element_type=F32)


def _dot_nt(a, b):
    return lax.dot_general(a, b, (((1,), (1,)), ((), ())), preferred_element_type=F32)


def _dot_tn(a, b):
    return lax.dot_general(a, b, (((0,), (0,)), ((), ())), preferred_element_type=F32)


def _dot_f32(a, b):
    return jnp.dot(a, b, preferred_element_type=F32, precision=lax.Precision.HIGHEST)


def _ln_kernel(x_ref, g_ref, b_ref, o_ref):
    o_ref[...] = _layer_norm(x_ref[...], g_ref[...], b_ref[...])


def _input_layer_norm(x, g, b, tm):
    T, D = x.shape
    return pl.pallas_call(
        _ln_kernel,
        out_shape=jax.ShapeDtypeStruct((T, D), F32),
        grid=(T // tm,),
        in_specs=[pl.BlockSpec((tm, D), lambda i: (i, 0)),
                  pl.BlockSpec((1, D), lambda i: (0, 0)),
                  pl.BlockSpec((1, D), lambda i: (0, 0))],
        out_specs=pl.BlockSpec((tm, D), lambda i: (i, 0)),
        compiler_params=_params(("parallel",)),
        name="ln_in",
    )(x, g.reshape(1, D), b.reshape(1, D))


C_FK = 0
C_MQK = C_FK + FOX_WIDTH
C_MV = C_MQK + 2 * MLSTM_WIDTH
C_MO = C_MV + MLSTM_WIDTH
C_SU = C_MO + MLSTM_WIDTH
C_SV = C_SU + SGU_WIDTH
C_GATE = C_SV + SGU_WIDTH
C_END = C_GATE + N_BRANCH * D_MODEL
PROJ_PIECE = 512


def _inproj_kernel(x_ref, wm_ref, bm_ref, wqvt_ref, bqvt_ref, ws_ref, bs_ref, wst_ref, bst_ref,
                   sg_ref, sb_ref, swt_ref, sbias_ref,
                   qt_ref, fk_ref, vt_ref, mqk_ref, mv_ref, mo_ref, yc_ref, gt_ref, sm_ref, smt_ref,
                   wbd_scr):
    tm = x_ref.shape[0]

    @pl.when(pl.program_id(0) == 0)
    def _():
        r = lax.broadcasted_iota(jnp.int32, (SGU_BLOCK, SGU_BLOCK), 0)
        c = lax.broadcasted_iota(jnp.int32, (SGU_BLOCK, SGU_BLOCK), 1)
        keep = jnp.logical_and(r // SGU_CHUNK == c // SGU_CHUNK, r >= c)
        for g in range(SGU_GROUPS):
            wbd_scr[g] = jnp.where(keep, swt_ref[g], 0.0).astype(BF16)

    xb = x_ref[...].astype(BF16)

    def proj(lo, width=PROJ_PIECE):
        return _dot(xb, wm_ref[:, lo:lo + width]) + bm_ref[:, lo:lo + width]

    fk_ref[...] = proj(C_FK).astype(BF16)
    qvt = _dot_nt(wqvt_ref[...], xb) + bqvt_ref[...]
    qt_ref[...] = (qvt[:FOX_WIDTH] * (FOX_HEAD_DIM ** -0.5)).astype(BF16)
    vt_ref[...] = qvt[FOX_WIDTH:].astype(BF16)
    for j in range(2):
        mqk_ref[:, j * PROJ_PIECE:(j + 1) * PROJ_PIECE] = proj(C_MQK + j * PROJ_PIECE).astype(BF16)
    mv_ref[...] = proj(C_MV).astype(BF16)
    mo_ref[...] = jax.nn.sigmoid(proj(C_MO)).astype(BF16)
    for j in range(N_BRANCH * D_MODEL // PROJ_PIECE):
        gt_ref[:, j * PROJ_PIECE:(j + 1) * PROJ_PIECE] = jax.nn.sigmoid(
            proj(C_GATE + j * PROJ_PIECE)).astype(BF16)

    sm = _dot(xb, ws_ref[...]) + bs_ref[...]
    lane = lax.broadcasted_iota(jnp.int32, sm.shape, 1)
    is_ls = jnp.logical_or(lane < SM_MI, jnp.logical_and(lane >= SM_MF, lane < SM_ROWS))
    sm_ref[...] = jnp.where(is_ls, _log_sigmoid(sm), sm)
    smt = _dot_nt(wst_ref[...], xb) + bst_ref[...]
    row = lax.broadcasted_iota(jnp.int32, smt.shape, 0)
    is_ls_t = jnp.logical_or(row < SM_MI, row >= SM_MF)
    smt_ref[...] = jnp.where(is_ls_t, _log_sigmoid(smt), smt)

    u = _gelu_tanh(proj(C_SU))
    v = _gelu_tanh(proj(C_SV))
    vn = _layer_norm(v, sg_ref[...], sb_ref[...]).astype(BF16)
    for rb in range(tm // SGU_BLOCK):
        rows = slice(rb * SGU_BLOCK, (rb + 1) * SGU_BLOCK)
        for g in range(SGU_GROUPS):
            cols = slice(g * LANES, (g + 1) * LANES)
            z = _dot(wbd_scr[g], vn[rows, cols]) + sbias_ref[rows, cols]
            yc_ref[rows, cols] = (u[rows, cols] * z).astype(BF16)


def _in_projection(x, p, tm):
    T, D = x.shape
    row = lambda w: pl.BlockSpec((tm, w), lambda i: (i, 0))
    out_shapes = (
        jax.ShapeDtypeStruct((FOX_WIDTH, T), BF16),
        jax.ShapeDtypeStruct((T, FOX_WIDTH), BF16),
        jax.ShapeDtypeStruct((FOX_WIDTH, T), BF16),
        jax.ShapeDtypeStruct((T, 2 * MLSTM_WIDTH), BF16),
        jax.ShapeDtypeStruct((T, MLSTM_WIDTH), BF16),
        jax.ShapeDtypeStruct((T, MLSTM_WIDTH), BF16),
        jax.ShapeDtypeStruct((T, SGU_WIDTH), BF16),
        jax.ShapeDtypeStruct((T, N_BRANCH * D_MODEL), BF16),
        jax.ShapeDtypeStruct((T, LANES), F32),
        jax.ShapeDtypeStruct((SM_ROWS, T), F32),
    )
    col = lambda h: pl.BlockSpec((h, tm), lambda i: (0, i))
    out_specs = (col(FOX_WIDTH), row(FOX_WIDTH), col(FOX_WIDTH),
                 row(2 * MLSTM_WIDTH), row(MLSTM_WIDTH), row(MLSTM_WIDTH), row(SGU_WIDTH),
                 row(N_BRANCH * D_MODEL), row(LANES), col(SM_ROWS))
    const = lambda shape: _resident(shape, lambda i: tuple(0 for _ in shape))
    in_specs = [
        row(D),
        const((D, C_END)), const((1, C_END)),
        const((2 * FOX_WIDTH, D)), const((2 * FOX_WIDTH, 1)),
        const((D, LANES)), const((1, LANES)),
        const((SM_ROWS, D)), const((SM_ROWS, 1)),
        const((1, SGU_WIDTH)), const((1, SGU_WIDTH)),
        const((SGU_GROUPS, SGU_BLOCK, SGU_BLOCK)),
        const((tm, SGU_WIDTH)),
    ]
    return pl.pallas_call(
        _inproj_kernel,
        out_shape=out_shapes,
        grid=(T // tm,),
        in_specs=in_specs,
        out_specs=out_specs,
        scratch_shapes=[pltpu.VMEM((SGU_GROUPS, SGU_BLOCK, SGU_BLOCK), BF16)],
        compiler_params=_params(("arbitrary",)),
        name="in_proj",
    )(x, p["wm"], p["bm"], p["wqvt"], p["bqvt"], p["ws"], p["bs"], p["wst"], p["bst"],
      p["sgu_g"], p["sgu_b"], p["sgu_wt"], p["sgu_bias"])


def _fcum_kernel(sm_ref, smt_ref, fcol_ref, frow_ref):
    S = sm_ref.shape[0]
    cb = min(CUMSUM_BLOCK, S)
    r = lax.broadcasted_iota(jnp.int32, (cb, cb), 0)
    c = lax.broadcasted_iota(jnp.int32, (cb, cb), 1)
    lower = (r >= c).astype(F32)
    upper = (r <= c).astype(F32)
    carry_c = jnp.zeros((1, LANES), F32)
    carry_r = jnp.zeros((SM_ROWS, 1), F32)
    for blk in range(S // cb):
        sl = slice(blk * cb, (blk + 1) * cb)
        fc = _dot_f32(lower, sm_ref[sl, :]) + carry_c
        fcol_ref[sl, :] = fc
        carry_c = fc[cb - 1:cb, :]
        fr = _dot_f32(smt_ref[:, sl], upper) + carry_r
        frow_ref[:, sl] = fr
        carry_r = fr[:, cb - 1:cb]


def _forget_cumsum(sm, smt, B, S):
    T = B * S
    return pl.pallas_call(
        _fcum_kernel,
        out_shape=(jax.ShapeDtypeStruct((T, LANES), F32), jax.ShapeDtypeStruct((SM_ROWS, T), F32)),
        grid=(B,),
        in_specs=[pl.BlockSpec((S, LANES), lambda b: (b, 0)),
                  pl.BlockSpec((SM_ROWS, S), lambda b: (0, b))],
        out_specs=(pl.BlockSpec((S, LANES), lambda b: (b, 0)),
                   pl.BlockSpec((SM_ROWS, S), lambda b: (0, b))),
        compiler_params=_params(("parallel",)),
        name="forget_cumsum",
    )(sm, smt)


N_SPLIT = 3


def _split3(f):
    hi = f.astype(BF16).astype(F32)
    r1 = f - hi
    mid = r1.astype(BF16).astype(F32)
    lo = r1 - mid
    return hi, mid, lo


def _fox_kernel(qt_ref, k_ref, vt_ref, fcol_ref, frow_ref, o_ref, kaug_scr, *, tq):
    S = k_ref.shape[0]
    pair = pl.program_id(1)
    qi = pl.program_id(2)

    @pl.when(qi == 0)
    def _():
        lane = lax.broadcasted_iota(jnp.int32, (S, LANES), 1)
        fc = fcol_ref[...]
        k = k_ref[...]
        for hh in range(2):
            f = jnp.sum(jnp.where(lane == 2 * pair + hh, fc, 0.0), axis=1, keepdims=True)
            hi, mid, lo = _split3(f)
            aug = jnp.where(lane < N_SPLIT, 1.0,
                            jnp.where(lane == N_SPLIT, -hi,
                                      jnp.where(lane == N_SPLIT + 1, -mid,
                                                jnp.where(lane == N_SPLIT + 2, -lo, 0.0))))
            kaug_scr[hh, :, 0:LANES] = k
            kaug_scr[hh, :, LANES:2 * LANES] = aug.astype(BF16)

    qt = qt_ref[...]
    sub = lax.broadcasted_iota(jnp.int32, (LANES, tq), 0)
    key = lax.broadcasted_iota(jnp.int32, (tq, tq), 0)
    qry = lax.broadcasted_iota(jnp.int32, (tq, tq), 1)
    causal = key <= qry

    rhs = []
    for hh in range(2):
        hi, mid, lo = _split3(frow_ref[hh])
        augq = jnp.where(sub == 0, hi,
                         jnp.where(sub == 1, mid,
                                   jnp.where(sub == 2, lo,
                                             jnp.where(sub < 2 * N_SPLIT, 1.0, 0.0))))
        in_head = (sub < FOX_HEAD_DIM) if hh == 0 else (sub >= FOX_HEAD_DIM)
        qm = jnp.where(in_head, qt, jnp.zeros_like(qt))
        rhs.append(jnp.concatenate([qm, augq.astype(BF16)], axis=0))

    def block(j, carry, masked):
        off = pl.multiple_of(j * tq, tq)
        vt_blk = vt_ref[:, pl.ds(off, tq)]
        new = []
        scores = [_dot(kaug_scr[hh, pl.ds(off, tq), :], rhs[hh]) for hh in range(2)]
        for hh in range(2):
            m, l, acc = carry[hh]
            s = scores[hh]
            if masked:
                s = jnp.where(causal, s, NEG_INF)
            m_new = jnp.maximum(m, jnp.max(s, axis=0, keepdims=True))
            p = jnp.exp(s - m_new)
            corr = jnp.exp(m - m_new)
            l = l * corr + jnp.sum(p, axis=0, keepdims=True)
            acc = acc * corr + _dot(vt_blk, p.astype(BF16))
            new.append((m_new, l, acc))
        return tuple(new)

    init_head = (jnp.full((1, tq), NEG_INF, F32), jnp.zeros((1, tq), F32), jnp.zeros((LANES, tq), F32))
    carry = lax.fori_loop(0, qi, functools.partial(block, masked=False), (init_head, init_head))
    carry = block(qi, carry, True)
    outs = [acc / l for (_, l, acc) in carry]
    out_t = jnp.where(sub < FOX_HEAD_DIM, outs[0], outs[1])
    o_ref[...] = out_t.T.astype(o_ref.dtype)


def _fox_attention(qt, fk, vt, fcol, frow, B, S, tq):
    T = B * S
    nq = S // tq
    n_pairs = FOX_HEADS // 2
    frow3 = frow.reshape(SM_ROWS, 1, T)
    return pl.pallas_call(
        functools.partial(_fox_kernel, tq=tq),
        out_shape=jax.ShapeDtypeStruct((T, FOX_WIDTH), BF16),
        grid=(B, n_pairs, nq),
        in_specs=[pl.BlockSpec((LANES, tq), lambda b, p, i: (p, b * nq + i)),
                  pl.BlockSpec((S, LANES), lambda b, p, i: (b, p)),
                  pl.BlockSpec((LANES, S), lambda b, p, i: (p, b)),
                  pl.BlockSpec((S, LANES), lambda b, p, i: (b, 0)),
                  pl.BlockSpec((2, 1, tq), lambda b, p, i: (p, 0, b * nq + i))],
        out_specs=pl.BlockSpec((tq, LANES), lambda b, p, i: (b * nq + i, p)),
        scratch_shapes=[pltpu.VMEM((2, S, 2 * LANES), BF16)],
        compiler_params=_params(("parallel", "parallel", "arbitrary")),
        name="fox_attention",
    )(qt, fk, vt, fcol, frow3)


def _mlstm_kernel(qk_ref, v_ref, og_ref, sm_ref, smt_ref, cw_ref, y_ref, c_scr, n_scr, m_scr, tail_scr):
    L = MLSTM_CHUNK
    dh = MLSTM_HEAD_DIM

    @pl.when(pl.program_id(1) == 0)
    def _():
        c_scr[...] = jnp.zeros_like(c_scr)
        n_scr[...] = jnp.zeros_like(n_scr)
        m_scr[...] = jnp.zeros_like(m_scr)
        tail_scr[...] = jnp.zeros_like(tail_scr)

    x = qk_ref[...].astype(F32)
    xe = jnp.concatenate([tail_scr[...], x], axis=0)
    cw = cw_ref[...]
    y = x * cw[MLSTM_CONV - 1:MLSTM_CONV, :]
    for k in range(1, MLSTM_CONV):
        shifted = pltpu.roll(xe, k, 0)[SUBLANES:, :]
        y = y + shifted * cw[MLSTM_CONV - 1 - k:MLSTM_CONV - k, :]
    tail_scr[...] = x[L - SUBLANES:, :]
    y = y * jax.nn.sigmoid(y)
    q_all = y[:, :MLSTM_WIDTH]
    k_all = y[:, MLSTM_WIDTH:] * (dh ** -0.5)

    sm = sm_ref[...]
    smt = smt_ref[...]
    r = lax.broadcasted_iota(jnp.int32, (L, L), 0)
    c = lax.broadcasted_iota(jnp.int32, (L, L), 1)
    causal = r >= c
    bcol_all = _dot_f32(causal.astype(F32), sm)
    brow_all = _dot_f32(smt, (r <= c).astype(F32))

    for h in range(MLSTM_HEADS):
        hs = slice(h * dh, (h + 1) * dh)
        bq = bcol_all[:, SM_MF + h:SM_MF + h + 1]
        li_c = sm[:, SM_MI + h:SM_MI + h + 1]
        br = brow_all[SM_MF + h:SM_MF + h + 1, :]
        li_r = smt[SM_MI + h:SM_MI + h + 1, :]
        m_prev = m_scr[h:h + 1, 0:1]
        b_last = bq[L - 1:L, :]

        d = jnp.where(causal, bq - br + li_r, NEG_INF)
        inter = bq + m_prev
        m_t = jnp.maximum(inter, jnp.max(d, axis=1, keepdims=True))
        w_intra = jnp.exp(d - m_t)
        w_inter = jnp.exp(inter - m_t)

        qh = q_all[:, hs]
        qb = qh.astype(BF16)
        kh = k_all[:, hs]
        vh = v_ref[:, hs]
        qk = _dot_nt(qb, kh.astype(BF16)) * w_intra
        c_prev = c_scr[h]
        n_prev = n_scr[h:h + 1, :]
        num = _dot(qk.astype(BF16), vh) + w_inter * _dot(qb, c_prev.astype(BF16))
        den = jnp.sum(qk, axis=1, keepdims=True) + w_inter * jnp.sum(qh * n_prev, axis=1, keepdims=True)
        h_c = num / jnp.maximum(jnp.abs(den), jnp.exp(-m_t))
        y_ref[:, hs] = (og_ref[:, hs].astype(F32) * h_c).astype(y_ref.dtype)

        g_c = b_last - bq + li_c
        m_new = jnp.maximum(b_last + m_prev, jnp.max(g_c, axis=0, keepdims=True))
        decay = jnp.exp(b_last + m_prev - m_new)
        kw = kh * jnp.exp(g_c - m_new)
        c_scr[h] = decay * c_prev + _dot_tn(kw.astype(BF16), vh)
        n_scr[h:h + 1, :] = decay * n_prev + jnp.sum(kw, axis=0, keepdims=True)
        m_scr[h:h + 1, :] = jnp.broadcast_to(m_new, (1, LANES))


def _mlstm(mqk, mv, mo, sm, smt, conv_w, B, S):
    T = B * S
    L = MLSTM_CHUNK
    nc = S // L
    row = lambda w: pl.BlockSpec((L, w), lambda b, c: (b * nc + c, 0))
    return pl.pallas_call(
        _mlstm_kernel,
        out_shape=jax.ShapeDtypeStruct((T, MLSTM_WIDTH), BF16),
        grid=(B, nc),
        in_specs=[row(2 * MLSTM_WIDTH), row(MLSTM_WIDTH), row(MLSTM_WIDTH), row(LANES),
                  pl.BlockSpec((SM_ROWS, L), lambda b, c: (0, b * nc + c)),
                  pl.BlockSpec((MLSTM_CONV, 2 * MLSTM_WIDTH), lambda b, c: (0, 0))],
        out_specs=row(MLSTM_WIDTH),
        scratch_shapes=[pltpu.VMEM((MLSTM_HEADS, MLSTM_HEAD_DIM, MLSTM_HEAD_DIM), F32),
                        pltpu.VMEM((SUBLANES, MLSTM_HEAD_DIM), F32),
                        pltpu.VMEM((SUBLANES, LANES), F32),
                        pltpu.VMEM((SUBLANES, 2 * MLSTM_WIDTH), F32)],
        compiler_params=_params(("parallel", "arbitrary")),
        name="mlstm",
    )(mqk, mv, mo, sm, smt, conv_w)


def _merge_kernel(ya_ref, yb_ref, yc_ref, gt_ref, x_ref, wb_ref, wo_ref, g_ref, b_ref, wr_ref, br_ref,
                  x1_ref, xs_ref, rt_ref, cnt_ref):
    tm = x_ref.shape[0]

    merged = None
    for n, y_ref in enumerate((ya_ref, yb_ref, yc_ref)):
        pr = _dot(y_ref[...], wb_ref[n]) * gt_ref[:, n * D_MODEL:(n + 1) * D_MODEL].astype(F32)
        merged = pr if merged is None else merged + pr
    mix = _dot(merged.astype(BF16), wo_ref[...])
    x1 = _layer_norm(DEEPNORM_ALPHA * x_ref[...] + mix, g_ref[...], b_ref[...])
    x1_ref[...] = x1

    logits = _dot(x1.astype(BF16), wr_ref[...]) + br_ref[...]
    lane = lax.broadcasted_iota(jnp.int32, (tm, LANES), 1)
    big = jnp.int32(LANES)
    glog = jnp.where(lane < N_GROUPS, logits, -jnp.inf)
    gmax = jnp.max(glog, axis=1, keepdims=True)
    g_top = jnp.min(jnp.where(glog == gmax, lane, big), axis=1, keepdims=True)
    p_g = 1.0 / jnp.sum(jnp.exp(glog - gmax), axis=1, keepdims=True)
    lo = ROUTER_EXPERT_LANE0 + EXPERTS_PER_GROUP * g_top
    el = jnp.where(jnp.logical_and(lane >= lo, lane < lo + EXPERTS_PER_GROUP), logits, -jnp.inf)
    m1 = jnp.max(el, axis=1, keepdims=True)
    i1 = jnp.min(jnp.where(el == m1, lane, big), axis=1, keepdims=True)
    el2 = jnp.where(lane == i1, -jnp.inf, el)
    m2 = jnp.max(el2, axis=1, keepdims=True)
    i2 = jnp.min(jnp.where(el2 == m2, lane, big), axis=1, keepdims=True)
    ratio = jnp.exp(m2 - m1)
    gate1 = p_g / (1.0 + ratio)
    gate2 = p_g * ratio / (1.0 + ratio)

    hit1 = lane == i1
    hit2 = lane == i2
    onehot = jnp.where(jnp.logical_or(hit1, hit2), 1.0, 0.0)
    r = lax.broadcasted_iota(jnp.int32, (tm, tm), 0)
    c = lax.broadcasted_iota(jnp.int32, (tm, tm), 1)
    before = jnp.where(r > c, 1.0, 0.0).astype(BF16)
    seen = _dot(before, onehot.astype(BF16))
    count = jnp.sum(onehot, axis=0, keepdims=True)
    chunks = jnp.floor((count + (MOE_ALIGN - 1)) * (1.0 / MOE_ALIGN))
    er = lax.broadcasted_iota(jnp.int32, (LANES, LANES), 0)
    ec = lax.broadcasted_iota(jnp.int32, (LANES, LANES), 1)
    earlier = jnp.where(er < ec, 1.0, 0.0)
    start = MOE_ALIGN * _dot_f32(jnp.broadcast_to(chunks, (SUBLANES, LANES)), earlier)[0:1, :]
    where_to = start + seen
    pos1 = jnp.sum(jnp.where(hit1, where_to, 0.0), axis=1, keepdims=True)
    pos2 = jnp.sum(jnp.where(hit2, where_to, 0.0), axis=1, keepdims=True)
    cnt_ref[0] = jnp.broadcast_to(count, (SUBLANES, LANES))

    e1 = (i1 - ROUTER_EXPERT_LANE0).astype(F32)
    e2 = (i2 - ROUTER_EXPERT_LANE0).astype(F32)
    rec = jnp.zeros((tm, LANES), F32)
    for pos, val in ((RT_E1, e1), (RT_E2, e2), (RT_P1, pos1), (RT_P2, pos2)):
        rec = jnp.where(lane == pos, val, rec)
    rt_ref[...] = rec[:, :RT_WIDTH]

    tag = jnp.zeros((tm, LANES), F32)
    pieces = _split3(gate1) + _split3(gate2) + (e1,)
    for pos, val in enumerate(pieces):
        tag = jnp.where(lane == pos, val, tag)
    xa = jnp.concatenate([x1.astype(BF16), tag.astype(BF16)], axis=1)
    n_local = xs_ref.shape[0]
    col = lax.broadcasted_iota(jnp.int32, (tm, n_local), 1)
    sel = jnp.logical_or(col == pos1.astype(jnp.int32), col == pos2.astype(jnp.int32))
    xs_ref[...] = _dot_tn(jnp.where(sel, 1.0, 0.0).astype(BF16), xa).astype(BF16)


def _local_rows(tm):
    worst = TOP_K * tm + N_EXPERTS * (MOE_ALIGN - 1)
    return -(-worst // LANES) * LANES


def _merge_project_route(ya, yb, yc, gates, x, p, tm):
    T, D = x.shape
    nt = T // tm
    n_local = _local_rows(tm)
    row = lambda w: pl.BlockSpec((tm, w), lambda i: (i, 0))
    const = lambda shape: _resident(shape, lambda i: tuple(0 for _ in shape))
    return pl.pallas_call(
        _merge_kernel,
        out_shape=(jax.ShapeDtypeStruct((T, D), F32),
                   jax.ShapeDtypeStruct((nt * n_local, XS_WIDTH), BF16),
                   jax.ShapeDtypeStruct((T, RT_WIDTH), F32),
                   jax.ShapeDtypeStruct((nt, SUBLANES, LANES), F32)),
        grid=(T // tm,),
        in_specs=[row(FOX_WIDTH), row(MLSTM_WIDTH), row(SGU_WIDTH), row(N_BRANCH * D), row(D),
                  const((N_BRANCH, FOX_WIDTH, D)), const((D, D)), const((1, D)), const((1, D)),
                  const((D, LANES)), const((1, LANES))],
        out_specs=(row(D), pl.BlockSpec((n_local, XS_WIDTH), lambda i: (i, 0)), row(RT_WIDTH),
                   pl.BlockSpec((1, SUBLANES, LANES), lambda i: (i, 0, 0))),
        compiler_params=_params(("parallel",)),
        name="merge_route",
    )(ya, yb, yc, gates, x, p["wb"], p["wo"], p["ln1_g"], p["ln1_b"], p["wr"], p["br"])


def _chunk_gather(src_hbm, idx_ref, dst, sem, n_chunks):
    for j in range(n_chunks):
        start = pl.multiple_of(idx_ref[0, 0, j] * MOE_ALIGN, MOE_ALIGN)
        pltpu.make_async_copy(src_hbm.at[pl.ds(start, MOE_ALIGN)], dst.at[pl.ds(j * MOE_ALIGN, MOE_ALIGN)],
                              sem).start(priority=j % 2)


def _chunk_gather_wait(src_hbm, dst, sem):
    n = dst.shape[0]
    pltpu.make_async_copy(src_hbm.at[pl.ds(0, n)], dst, sem).wait()


def _expert_kernel(be_ref, nu_ref, src0_ref, src_next_ref, xs_hbm, wg_ref, wu_ref, wd_ref, y_ref,
                   xbuf, wg_scr, wu_scr, wd_scr, gsem):
    i = pl.program_id(0)
    n_used = nu_ref[0]
    slot = lax.rem(i, 2)
    active = i < n_used

    @pl.when(jnp.logical_and(i == 0, active))
    def _():
        _chunk_gather(xs_hbm, src0_ref, xbuf.at[0], gsem.at[0], MOE_CHUNKS)

    @pl.when(i + 1 < n_used)
    def _():
        _chunk_gather(xs_hbm, src_next_ref, xbuf.at[1 - slot], gsem.at[1 - slot], MOE_CHUNKS)

    new_expert = jnp.logical_or(i == 0, be_ref[i] != be_ref[jnp.maximum(i - 1, 0)])

    @pl.when(jnp.logical_and(active, new_expert))
    def _():
        wg_scr[...] = wg_ref[...].astype(BF16)
        wu_scr[...] = wu_ref[...].astype(BF16)
        wd_scr[...] = wd_ref[...].astype(BF16)

    @pl.when(active)
    def _():
        _chunk_gather_wait(xs_hbm, xbuf.at[slot], gsem.at[slot])
        xa = xbuf[slot]
        xb = xa[:, :D_MODEL]
        tag = xa[:, D_MODEL:].astype(F32)
        lane = lax.broadcasted_iota(jnp.int32, tag.shape, 1)
        pick = lambda lo, hi: jnp.sum(jnp.where(jnp.logical_and(lane >= lo, lane < hi), tag, 0.0),
                                      axis=1, keepdims=True)
        gate1 = pick(XS_G1, XS_G1 + N_SPLIT)
        gate2 = pick(XS_G2, XS_G2 + N_SPLIT)
        first = pick(XS_E1, XS_E1 + 1) == be_ref[i].astype(F32)
        g = _dot(xb, wg_scr[...])
        up = _dot(xb, wu_scr[...])
        hid = (g * jax.nn.sigmoid(g) * up).astype(BF16)
        y_ref[...] = (jnp.where(first, gate1, gate2) * _dot(hid, wd_scr[...])).astype(y_ref.dtype)

    @pl.when(jnp.logical_not(active))
    def _():
        y_ref[...] = jnp.zeros_like(y_ref)


def _expert_mlp(xs, chunk_src, block_expert, n_used, w_gate, w_up, w_down, layer):
    nb = chunk_src.shape[0]
    R = MOE_ROWS
    wsel = lambda i, be, nu: (layer, be[i], 0, 0)
    idx_spec = lambda f: pl.BlockSpec((1, 1, MOE_CHUNKS), f, memory_space=pltpu.SMEM)
    return pl.pallas_call(
        _expert_kernel,
        out_shape=jax.ShapeDtypeStruct((nb * R, D_MODEL), BF16),
        grid_spec=pltpu.PrefetchScalarGridSpec(
            num_scalar_prefetch=2,
            grid=(nb,),
            in_specs=[idx_spec(lambda i, be, nu: (0, 0, 0)),
                      idx_spec(lambda i, be, nu: (jnp.minimum(i + 1, nb - 1), 0, 0)),
                      pl.BlockSpec(memory_space=pl.ANY),
                      pl.BlockSpec((None, None, D_MODEL, D_EXPERT), wsel),
                      pl.BlockSpec((None, None, D_MODEL, D_EXPERT), wsel),
                      pl.BlockSpec((None, None, D_EXPERT, D_MODEL), wsel)],
            out_specs=pl.BlockSpec((R, D_MODEL), lambda i, be, nu: (i, 0)),
            scratch_shapes=[pltpu.VMEM((2, R, XS_WIDTH), BF16),
                            pltpu.VMEM((D_MODEL, D_EXPERT), BF16),
                            pltpu.VMEM((D_MODEL, D_EXPERT), BF16),
                            pltpu.VMEM((D_EXPERT, D_MODEL), BF16),
                            pltpu.SemaphoreType.DMA((2,))]),
        compiler_params=_params(("arbitrary",)),
        name="moe_experts",
    )(block_expert, n_used, chunk_src, chunk_src, xs, w_gate, w_up, w_down)


def _combine_kernel(src0_ref, src_next_ref, ys_hbm, rt_ref, x1_ref, g_ref, b_ref, o_ref, ybuf, sem):
    i = pl.program_id(0)
    nt = pl.num_programs(0)
    slot = lax.rem(i, 2)
    th = x1_ref.shape[0]
    n_local = ybuf.shape[1]
    n_chunks = n_local // MOE_ALIGN

    @pl.when(i == 0)
    def _():
        _chunk_gather(ys_hbm, src0_ref, ybuf.at[0], sem.at[0], n_chunks)

    @pl.when(i + 1 < nt)
    def _():
        _chunk_gather(ys_hbm, src_next_ref, ybuf.at[1 - slot], sem.at[1 - slot], n_chunks)

    _chunk_gather_wait(ys_hbm, ybuf.at[slot], sem.at[slot])
    rt = rt_ref[...]
    pos1 = rt[:, RT_P1:RT_P1 + 1].astype(jnp.int32)
    pos2 = rt[:, RT_P2:RT_P2 + 1].astype(jnp.int32)
    col = lax.broadcasted_iota(jnp.int32, (th, n_local), 1)
    two_hot = jnp.where(jnp.logical_or(col == pos1, col == pos2), 1.0, 0.0).astype(BF16)
    ffn = _dot(two_hot, ybuf[slot])
    o_ref[...] = _layer_norm(DEEPNORM_ALPHA * x1_ref[...] + ffn, g_ref[...], b_ref[...])


def _combine(ys, chunk_src, route, x1, g, b, th):
    T, D = x1.shape
    nt = T // th
    n_chunks = chunk_src.shape[-1]
    idx_spec = lambda f: pl.BlockSpec((1, 1, n_chunks), f, memory_space=pltpu.SMEM)
    return pl.pallas_call(
        _combine_kernel,
        out_shape=jax.ShapeDtypeStruct((T, D), F32),
        grid=(nt,),
        in_specs=[idx_spec(lambda i: (0, 0, 0)),
                  idx_spec(lambda i: (jnp.minimum(i + 1, nt - 1), 0, 0)),
                  pl.BlockSpec(memory_space=pl.ANY),
                  pl.BlockSpec((th, RT_WIDTH), lambda i: (i, 0)),
                  pl.BlockSpec((th, D), lambda i: (i, 0)),
                  pl.BlockSpec((1, D), lambda i: (0, 0)),
                  pl.BlockSpec((1, D), lambda i: (0, 0))],
        out_specs=pl.BlockSpec((th, D), lambda i: (i, 0)),
        scratch_shapes=[pltpu.VMEM((2, n_chunks * MOE_ALIGN, D), BF16), pltpu.SemaphoreType.DMA((2,))],
        compiler_params=_params(("arbitrary",)),
        name="moe_combine",
    )(chunk_src, chunk_src, ys, route, x1, g, b)


def _moe(x1, xs, route, counts, p, layer, tm):
    T = x1.shape[0]
    nt = T // tm
    lc = _local_rows(tm) // MOE_ALIGN
    i32 = jnp.int32
    cnt = counts[:, 0, ROUTER_EXPERT_LANE0:ROUTER_EXPERT_LANE0 + N_EXPERTS].astype(i32)
    ch = (cnt + MOE_ALIGN - 1) // MOE_ALIGN
    loff = jnp.cumsum(ch, axis=1) - ch
    eblk = (jnp.sum(ch, axis=0) + MOE_CHUNKS - 1) // MOE_CHUNKS
    eend = jnp.cumsum(eblk)
    run_start = (eend - eblk)[None, :] * MOE_CHUNKS + (jnp.cumsum(ch, axis=0) - ch)
    nb = nt * lc // MOE_CHUNKS + N_EXPERTS
    rs = run_start.T.reshape(-1)
    c = jnp.arange(nb * MOE_CHUNKS, dtype=i32)
    run = jnp.sum((rs[None, :] <= c[:, None]).astype(i32), axis=1) - 1
    off = c - rs[run]
    tile_of_run = jnp.tile(jnp.arange(nt, dtype=i32), N_EXPERTS)
    src = jnp.where(off < ch.T.reshape(-1)[run], tile_of_run[run] * lc + loff.T.reshape(-1)[run] + off, 0)
    block_expert = jnp.minimum(
        jnp.sum((eend[None, :] <= jnp.arange(nb, dtype=i32)[:, None]).astype(i32), axis=1), N_EXPERTS - 1)
    n_used = eend[-1:].astype(i32)
    ys = _expert_mlp(xs, src.reshape(nb, 1, MOE_CHUNKS), block_expert, n_used,
                     p["w_gate"], p["w_up"], p["w_down"], layer)
    j = jnp.arange(lc, dtype=i32)
    e_of = jnp.sum((loff[:, None, :] <= j[None, :, None]).astype(i32), axis=2) - 1
    take = lambda a: jnp.take_along_axis(a, e_of, axis=1)
    off = j[None, :] - take(loff)
    back = jnp.where(off < take(ch), take(run_start) + off, 0)
    return _combine(ys, back.reshape(nt, 1, lc), route, x1, p["ln2_g"], p["ln2_b"], tm)


def _layer_params(l, w_in, b_in, conv_w, sgu_ln_g, sgu_ln_b, sgu_w, sgu_b, w_branch, w_out, ln1_g, ln1_b,
                  w_group, b_group, w_router, b_router, w_gate, w_up, w_down, ln2_g, ln2_b, tm):
    offs = [0]
    for s in IN_SIZES:
        offs.append(offs[-1] + s)
    wt, b = w_in[l].T, b_in[l]
    seg = lambda a, i: a[offs[i]:offs[i + 1]]
    fox_wt, fox_b = seg(wt, 0), seg(b, 0)
    fq_wt, fk_wt, fv_wt = (fox_wt[j * FOX_WIDTH:(j + 1) * FOX_WIDTH] for j in range(3))
    fq_b, fk_b, fv_b = (fox_b[j * FOX_WIDTH:(j + 1) * FOX_WIDTH] for j in range(3))
    wmt = jnp.concatenate([fk_wt, seg(wt, 2), seg(wt, 3), seg(wt, 4), seg(wt, 6), seg(wt, 7)], axis=0)
    bm = jnp.concatenate([fk_b, seg(b, 2), seg(b, 3), seg(b, 4), seg(b, 6), seg(b, 7)])
    n_small = IN_SIZES[1] + IN_SIZES[5]
    wst = jnp.concatenate([seg(wt, 1), seg(wt, 5)], axis=0).astype(BF16)
    ws = jnp.concatenate([wst, jnp.zeros((LANES - n_small, D_MODEL), BF16)], axis=0).T
    bs = jnp.concatenate([seg(b, 1), seg(b, 5), jnp.zeros((LANES - n_small,), F32)])
    reps = SGU_BLOCK // SGU_CHUNK
    sgu_bias = jnp.broadcast_to(sgu_b[l].T[:, :, None], (SGU_CHUNK, SGU_GROUPS, SGU_WIDTH // SGU_GROUPS))
    sgu_bias = jnp.tile(sgu_bias.reshape(SGU_CHUNK, SGU_WIDTH), (tm // SGU_CHUNK, 1))
    n_route = N_GROUPS + N_EXPERTS
    wr = jnp.concatenate([w_group[l], w_router[l], jnp.zeros((D_MODEL, LANES - n_route), F32)], axis=1)
    br = jnp.concatenate([b_group[l], b_router[l], jnp.zeros((LANES - n_route,), F32)])
    return {
        "wm": wmt.astype(BF16).T, "bm": bm.reshape(1, -1),
        "wqvt": jnp.concatenate([fq_wt, fv_wt], axis=0).astype(BF16),
        "bqvt": jnp.concatenate([fq_b, fv_b]).reshape(-1, 1),
        "ws": ws, "bs": bs.reshape(1, -1),
        "wst": wst, "bst": bs[:SM_ROWS].reshape(-1, 1),
        "conv_w": conv_w[l],
        "sgu_g": sgu_ln_g[l].reshape(1, -1), "sgu_b": sgu_ln_b[l].reshape(1, -1),
        "sgu_wt": jnp.tile(sgu_w[l], (1, reps, reps)), "sgu_bias": sgu_bias,
        "wb": w_branch[l].astype(BF16), "wo": w_out[l].astype(BF16),
        "ln1_g": ln1_g[l].reshape(1, -1), "ln1_b": ln1_b[l].reshape(1, -1),
        "wr": wr.astype(BF16), "br": br.reshape(1, -1),
        "w_gate": w_gate, "w_up": w_up, "w_down": w_down,
        "ln2_g": ln2_g[l].reshape(1, -1), "ln2_b": ln2_b[l].reshape(1, -1),
    }


def _tiles(B, S):
    T = B * S
    tm = 512 if T % 512 == 0 else 256
    tq = 512 if S % 512 == 0 else 256
    return tm, tq


def kernel(x, ln_in_g, ln_in_b, w_in, b_in, conv_w, sgu_ln_g, sgu_ln_b, sgu_w, sgu_b, w_branch, w_out,
           ln1_g, ln1_b, w_group, b_group, w_router, b_router, w_gate, w_up, w_down, ln2_g, ln2_b):
    B, S, D = x.shape
    assert D == D_MODEL and S % SGU_BLOCK == 0
    T = B * S
    tm, tq = _tiles(B, S)
    h = _input_layer_norm(x.reshape(T, D), ln_in_g, ln_in_b, tm)
    for l in range(DEPTH):
        p = _layer_params(l, w_in, b_in, conv_w, sgu_ln_g, sgu_ln_b, sgu_w, sgu_b, w_branch, w_out, ln1_g,
                          ln1_b, w_group, b_group, w_router, b_router, w_gate, w_up, w_down, ln2_g, ln2_b, tm)
        qt, fk, vt, mqk, mv, mo, yc, gates, sm, smt = _in_projection(h, p, tm)
        fcol, frow = _forget_cumsum(sm, smt, B, S)
        ya = _fox_attention(qt, fk, vt, fcol, frow, B, S, tq)
        yb = _mlstm(mqk, mv, mo, sm, smt, p["conv_w"], B, S)
        x1, xs, route, counts = _merge_project_route(ya, yb, yc, gates, h, p, tm)
        h = _moe(x1, xs, route, counts, p, l, tm)
    return h.reshape(B, S, D)
```

```python
import functools

import jax
import jax.numpy as jnp
from jax import lax
from jax.experimental import pallas as pl
from jax.experimental.pallas import tpu as pltpu

F32 = jnp.float32
BF16 = jnp.bfloat16

D_MODEL = 1024
DEPTH = 2
FOX_HEADS = 8
FOX_HEAD_DIM = 64
FOX_WIDTH = FOX_HEADS * FOX_HEAD_DIM
MLSTM_HEADS = 4
MLSTM_HEAD_DIM = 128
MLSTM_WIDTH = MLSTM_HEADS * MLSTM_HEAD_DIM
MLSTM_CHUNK = 128
MLSTM_CONV = 4
SGU_GROUPS = 4
SGU_WIDTH = 512
SGU_CHUNK = 128
N_BRANCH = 3
N_GROUPS = 4
EXPERTS_PER_GROUP = 8
N_EXPERTS = N_GROUPS * EXPERTS_PER_GROUP
TOP_K = 2
D_EXPERT = 512
DEEPNORM_ALPHA = (2 * DEPTH) ** 0.25
LN_EPS = 1e-5
NEG_INF = -1e30

IN_SIZES = (3 * FOX_WIDTH, FOX_HEADS, 2 * MLSTM_WIDTH, MLSTM_WIDTH, MLSTM_WIDTH,
            2 * MLSTM_HEADS, 2 * SGU_WIDTH, N_BRANCH * D_MODEL)

LANES = 128
SUBLANES = 8
VMEM_LIMIT_BYTES = 56 * 1024 * 1024

SM_FOXF = 0
SM_MI = 8
SM_MF = 12
SM_ROWS = 16

RT_E1, RT_E2, RT_P1, RT_P2 = 0, 1, 2, 3
RT_WIDTH = 8
ROUTER_EXPERT_LANE0 = N_GROUPS

MOE_ROWS = 256
MOE_ALIGN = 16
MOE_CHUNKS = MOE_ROWS // MOE_ALIGN
EXPERT_BUFFERS = 3
XS_G1, XS_G2, XS_E1 = 0, 3, 6
XS_WIDTH = D_MODEL + LANES
SGU_BLOCK = 256
CUMSUM_BLOCK = 256


def _params(sem, vmem=VMEM_LIMIT_BYTES):
    return pltpu.CompilerParams(dimension_semantics=sem, vmem_limit_bytes=vmem)


def _resident(shape, index_map):
    return pl.BlockSpec(shape, index_map, pipeline_mode=pl.Buffered(1))


def _layer_norm(x, g, b):
    mu = jnp.mean(x, axis=-1, keepdims=True)
    xc = x - mu
    var = jnp.mean(xc * xc, axis=-1, keepdims=True)
    return xc * lax.rsqrt(var + LN_EPS) * g + b


def _log_sigmoid(x):
    return jnp.minimum(x, 0.0) - jnp.log1p(jnp.exp(-jnp.abs(x)))


def _gelu_tanh(x):
    return 0.5 * x * (1.0 + jnp.tanh(0.7978845608028654 * (x + 0.044715 * x * x * x)))


def _dot(a, b):
    return jnp.dot(a, b, preferred_element_type=F32)


def _dot_nt(a, b):
    return lax.dot_general(a, b, (((1,), (1,)), ((), ())), preferred_element_type=F32)


def _dot_tn(a, b):
    return lax.dot_general(a, b, (((0,), (0,)), ((), ())), preferred_element_type=F32)


def _dot_f32(a, b):
    return jnp.dot(a, b, preferred_element_type=F32, precision=lax.Precision.HIGHEST)


def _ln_kernel(x_ref, g_ref, b_ref, o_ref):
    o_ref[...] = _layer_norm(x_ref[...], g_ref[...], b_ref[...])


def _input_layer_norm(x, g, b, tm):
    T, D = x.shape
    return pl.pallas_call(
        _ln_kernel,
        out_shape=jax.ShapeDtypeStruct((T, D), F32),
        grid=(T // tm,),
        in_specs=[pl.BlockSpec((tm, D), lambda i: (i, 0)),
                  pl.BlockSpec((1, D), lambda i: (0, 0)),
                  pl.BlockSpec((1, D), lambda i: (0, 0))],
        out_specs=pl.BlockSpec((tm, D), lambda i: (i, 0)),
        compiler_params=_params(("parallel",)),
        name="ln_in",
    )(x, g.reshape(1, D), b.reshape(1, D))


C_FK = 0
C_MQK = C_FK + FOX_WIDTH
C_MV = C_MQK + 2 * MLSTM_WIDTH
C_MO = C_MV + MLSTM_WIDTH
C_SU = C_MO + MLSTM_WIDTH
C_SV = C_SU + SGU_WIDTH
C_GATE = C_SV + SGU_WIDTH
C_END = C_GATE + N_BRANCH * D_MODEL
PROJ_PIECE = 512


def _inproj_kernel(x_ref, wm_ref, bm_ref, wqvt_ref, bqvt_ref, ws_ref, bs_ref, wst_ref, bst_ref,
                   sg_ref, sb_ref, swt_ref, sbias_ref,
                   qt_ref, fk_ref, vt_ref, mqk_ref, mv_ref, mo_ref, yc_ref, gt_ref, sm_ref, smt_ref,
                   wbd_scr):
    tm = x_ref.shape[0]

    @pl.when(pl.program_id(0) == 0)
    def _():
        r = lax.broadcasted_iota(jnp.int32, (SGU_BLOCK, SGU_BLOCK), 0)
        c = lax.broadcasted_iota(jnp.int32, (SGU_BLOCK, SGU_BLOCK), 1)
        keep = jnp.logical_and(r // SGU_CHUNK == c // SGU_CHUNK, r >= c)
        for g in range(SGU_GROUPS):
            wbd_scr[g] = jnp.where(keep, swt_ref[g], 0.0).astype(BF16)

    xb = x_ref[...].astype(BF16)

    def proj(lo, width=PROJ_PIECE):
        return _dot(xb, wm_ref[:, lo:lo + width]) + bm_ref[:, lo:lo + width]

    fk_ref[...] = proj(C_FK).astype(BF16)
    qvt = _dot_nt(wqvt_ref[...], xb) + bqvt_ref[...]
    qt_ref[...] = (qvt[:FOX_WIDTH] * (LOG2E * FOX_HEAD_DIM ** -0.5)).astype(BF16)
    vt_ref[...] = qvt[FOX_WIDTH:].astype(BF16)
    for j in range(2):
        mqk_ref[:, j * PROJ_PIECE:(j + 1) * PROJ_PIECE] = proj(C_MQK + j * PROJ_PIECE).astype(BF16)
    mv_ref[...] = proj(C_MV).astype(BF16)
    mo_ref[...] = jax.nn.sigmoid(proj(C_MO)).astype(BF16)
    for j in range(N_BRANCH * D_MODEL // PROJ_PIECE):
        gt_ref[:, j * PROJ_PIECE:(j + 1) * PROJ_PIECE] = jax.nn.sigmoid(
            proj(C_GATE + j * PROJ_PIECE)).astype(BF16)

    sm = _dot(xb, ws_ref[...]) + bs_ref[...]
    lane = lax.broadcasted_iota(jnp.int32, sm.shape, 1)
    is_ls = jnp.logical_or(lane < SM_MI, jnp.logical_and(lane >= SM_MF, lane < SM_ROWS))
    sm_ref[...] = jnp.where(is_ls, _log_sigmoid(sm), sm)
    smt = _dot_nt(wst_ref[...], xb) + bst_ref[...]
    row = lax.broadcasted_iota(jnp.int32, smt.shape, 0)
    is_ls_t = jnp.logical_or(row < SM_MI, row >= SM_MF)
    smt_ref[...] = jnp.where(is_ls_t, _log_sigmoid(smt), smt)

    u = _gelu_tanh(proj(C_SU))
    v = _gelu_tanh(proj(C_SV))
    vn = _layer_norm(v, sg_ref[...], sb_ref[...]).astype(BF16)
    for rb in range(tm // SGU_BLOCK):
        rows = slice(rb * SGU_BLOCK, (rb + 1) * SGU_BLOCK)
        for g in range(SGU_GROUPS):
            cols = slice(g * LANES, (g + 1) * LANES)
            z = _dot(wbd_scr[g], vn[rows, cols]) + sbias_ref[rows, cols]
            yc_ref[rows, cols] = (u[rows, cols] * z).astype(BF16)


def _in_projection(x, p, tm):
    T, D = x.shape
    row = lambda w: pl.BlockSpec((tm, w), lambda i: (i, 0))
    out_shapes = (
        jax.ShapeDtypeStruct((FOX_WIDTH, T), BF16),
        jax.ShapeDtypeStruct((T, FOX_WIDTH), BF16),
        jax.ShapeDtypeStruct((FOX_WIDTH, T), BF16),
        jax.ShapeDtypeStruct((T, 2 * MLSTM_WIDTH), BF16),
        jax.ShapeDtypeStruct((T, MLSTM_WIDTH), BF16),
        jax.ShapeDtypeStruct((T, MLSTM_WIDTH), BF16),
        jax.ShapeDtypeStruct((T, SGU_WIDTH), BF16),
        jax.ShapeDtypeStruct((T, N_BRANCH * D_MODEL), BF16),
        jax.ShapeDtypeStruct((T, LANES), F32),
        jax.ShapeDtypeStruct((SM_ROWS, T), F32),
    )
    col = lambda h: pl.BlockSpec((h, tm), lambda i: (0, i))
    out_specs = (col(FOX_WIDTH), row(FOX_WIDTH), col(FOX_WIDTH),
                 row(2 * MLSTM_WIDTH), row(MLSTM_WIDTH), row(MLSTM_WIDTH), row(SGU_WIDTH),
                 row(N_BRANCH * D_MODEL), row(LANES), col(SM_ROWS))
    const = lambda shape: _resident(shape, lambda i: tuple(0 for _ in shape))
    in_specs = [
        row(D),
        const((D, C_END)), const((1, C_END)),
        const((2 * FOX_WIDTH, D)), const((2 * FOX_WIDTH, 1)),
        const((D, LANES)), const((1, LANES)),
        const((SM_ROWS, D)), const((SM_ROWS, 1)),
        const((1, SGU_WIDTH)), const((1, SGU_WIDTH)),
        const((SGU_GROUPS, SGU_BLOCK, SGU_BLOCK)),
        const((tm, SGU_WIDTH)),
    ]
    return pl.pallas_call(
        _inproj_kernel,
        out_shape=out_shapes,
        grid=(T // tm,),
        in_specs=in_specs,
        out_specs=out_specs,
        scratch_shapes=[pltpu.VMEM((SGU_GROUPS, SGU_BLOCK, SGU_BLOCK), BF16)],
        compiler_params=_params(("arbitrary",)),
        name="in_proj",
    )(x, p["wm"], p["bm"], p["wqvt"], p["bqvt"], p["ws"], p["bs"], p["wst"], p["bst"],
      p["sgu_g"], p["sgu_b"], p["sgu_wt"], p["sgu_bias"])


def _fcum_kernel(sm_ref, smt_ref, fcol_ref, frow_ref):
    S = sm_ref.shape[0]
    cb = min(CUMSUM_BLOCK, S)
    r = lax.broadcasted_iota(jnp.int32, (cb, cb), 0)
    c = lax.broadcasted_iota(jnp.int32, (cb, cb), 1)
    lower = (r >= c).astype(F32)
    upper = (r <= c).astype(F32)
    carry_c = jnp.zeros((1, LANES), F32)
    carry_r = jnp.zeros((SM_ROWS, 1), F32)
    for blk in range(S // cb):
        sl = slice(blk * cb, (blk + 1) * cb)
        fc = _dot_f32(lower, sm_ref[sl, :]) + carry_c
        fcol_ref[sl, :] = fc
        carry_c = fc[cb - 1:cb, :]
        fr = _dot_f32(smt_ref[:, sl], upper) + carry_r
        frow_ref[:, sl] = fr
        carry_r = fr[:, cb - 1:cb]


def _forget_cumsum(sm, smt, B, S):
    T = B * S
    return pl.pallas_call(
        _fcum_kernel,
        out_shape=(jax.ShapeDtypeStruct((T, LANES), F32), jax.ShapeDtypeStruct((SM_ROWS, T), F32)),
        grid=(B,),
        in_specs=[pl.BlockSpec((S, LANES), lambda b: (b, 0)),
                  pl.BlockSpec((SM_ROWS, S), lambda b: (0, b))],
        out_specs=(pl.BlockSpec((S, LANES), lambda b: (b, 0)),
                   pl.BlockSpec((SM_ROWS, S), lambda b: (0, b))),
        compiler_params=_params(("parallel",)),
        name="forget_cumsum",
    )(sm, smt)


N_SPLIT = 3
LOG2E = 1.4426950408889634


def _split3(f):
    hi = f.astype(BF16).astype(F32)
    r1 = f - hi
    mid = r1.astype(BF16).astype(F32)
    lo = r1 - mid
    return hi, mid, lo


def _fox_kernel(qt_ref, k_ref, vt_ref, fcol_ref, frow_ref, o_ref, kaug_scr, s_scr, m_scr, l_scr, acc_scr, *, tq):
    S = k_ref.shape[0]
    pair = pl.program_id(1)
    qi = pl.program_id(2)

    @pl.when(qi == 0)
    def _():
        lane = lax.broadcasted_iota(jnp.int32, (S, LANES), 1)
        fc = fcol_ref[...]
        k = k_ref[...]
        for hh in range(2):
            f = jnp.sum(jnp.where(lane == 2 * pair + hh, fc, 0.0), axis=1, keepdims=True)
            hi, mid, lo = _split3(LOG2E * f)
            aug = jnp.where(lane < N_SPLIT, 1.0,
                            jnp.where(lane == N_SPLIT, -hi,
                                      jnp.where(lane == N_SPLIT + 1, -mid,
                                                jnp.where(lane == N_SPLIT + 2, -lo, 0.0))))
            kaug_scr[hh, :, 0:LANES] = k
            kaug_scr[hh, :, LANES:2 * LANES] = aug.astype(BF16)

    qt = qt_ref[...]
    sub = lax.broadcasted_iota(jnp.int32, (LANES, tq), 0)
    key = lax.broadcasted_iota(jnp.int32, (tq, tq), 0)
    qry = lax.broadcasted_iota(jnp.int32, (tq, tq), 1)
    causal = key <= qry

    rhs = []
    for hh in range(2):
        hi, mid, lo = _split3(LOG2E * frow_ref[hh])
        augq = jnp.where(sub == 0, hi,
                         jnp.where(sub == 1, mid,
                                   jnp.where(sub == 2, lo,
                                             jnp.where(sub < 2 * N_SPLIT, 1.0, 0.0))))
        in_head = (sub < FOX_HEAD_DIM) if hh == 0 else (sub >= FOX_HEAD_DIM)
        qm = jnp.where(in_head, qt, jnp.zeros_like(qt))
        rhs.append(jnp.concatenate([qm, augq.astype(BF16)], axis=0))

    def scores(j, hh, slot):
        off = pl.multiple_of(j * tq, tq)
        s_scr[slot, hh] = _dot(kaug_scr[hh, pl.ds(off, tq), :], rhs[hh])

    def consume(j, hh, slot, masked):
        off = pl.multiple_of(j * tq, tq)
        s = s_scr[slot, hh]
        if masked:
            s = jnp.where(causal, s, NEG_INF)
        m = m_scr[hh]
        m_new = jnp.maximum(m, jnp.max(s, axis=0, keepdims=True))
        p = jnp.exp2(s - m_new)
        corr = jnp.exp2(m - m_new)
        m_scr[hh] = m_new
        l_scr[hh] = l_scr[hh] * corr + jnp.sum(p, axis=0, keepdims=True)
        acc_scr[hh] = acc_scr[hh] * corr + _dot(vt_ref[:, pl.ds(off, tq)], p.astype(BF16))

    for hh in range(2):
        m_scr[hh] = jnp.full((1, tq), NEG_INF, F32)
        l_scr[hh] = jnp.zeros((1, tq), F32)
        acc_scr[hh] = jnp.zeros((LANES, tq), F32)
        scores(0, hh, 0)

    def two_blocks(jj, carry):
        j0 = 2 * jj
        for step in range(2):
            for hh in range(2):
                scores(j0 + step + 1, hh, 1 - step)
                consume(j0 + step, hh, step, False)
        return carry

    lax.fori_loop(0, qi // 2, two_blocks, 0)
    odd = lax.rem(qi, 2) == 1

    @pl.when(odd)
    def _():
        for hh in range(2):
            scores(qi, hh, 1)
            consume(qi - 1, hh, 0, False)
        for hh in range(2):
            consume(qi, hh, 1, True)

    @pl.when(jnp.logical_not(odd))
    def _():
        for hh in range(2):
            consume(qi, hh, 0, True)

    outs = [acc_scr[hh] / l_scr[hh] for hh in range(2)]
    out_t = jnp.where(sub < FOX_HEAD_DIM, outs[0], outs[1])
    o_ref[...] = out_t.T.astype(o_ref.dtype)


def _fox_attention(qt, fk, vt, fcol, frow, B, S, tq):
    T = B * S
    nq = S // tq
    n_pairs = FOX_HEADS // 2
    frow3 = frow.reshape(SM_ROWS, 1, T)
    return pl.pallas_call(
        functools.partial(_fox_kernel, tq=tq),
        out_shape=jax.ShapeDtypeStruct((T, FOX_WIDTH), BF16),
        grid=(B, n_pairs, nq),
        in_specs=[pl.BlockSpec((LANES, tq), lambda b, p, i: (p, b * nq + i)),
                  pl.BlockSpec((S, LANES), lambda b, p, i: (b, p)),
                  pl.BlockSpec((LANES, S), lambda b, p, i: (p, b)),
                  pl.BlockSpec((S, LANES), lambda b, p, i: (b, 0)),
                  pl.BlockSpec((2, 1, tq), lambda b, p, i: (p, 0, b * nq + i))],
        out_specs=pl.BlockSpec((tq, LANES), lambda b, p, i: (b * nq + i, p)),
        scratch_shapes=[pltpu.VMEM((2, S, 2 * LANES), BF16),
                        pltpu.VMEM((2, 2, tq, tq), F32),
                        pltpu.VMEM((2, 1, tq), F32),
                        pltpu.VMEM((2, 1, tq), F32),
                        pltpu.VMEM((2, LANES, tq), F32)],
        compiler_params=_params(("parallel", "parallel", "arbitrary")),
        name="fox_attention",
    )(qt, fk, vt, fcol, frow3)


def _mlstm_kernel(qk_ref, v_ref, og_ref, sm_ref, smt_ref, cw_ref, y_ref, c_scr, n_scr, m_scr, tail_scr):
    L = MLSTM_CHUNK
    dh = MLSTM_HEAD_DIM

    @pl.when(pl.program_id(1) == 0)
    def _():
        c_scr[...] = jnp.zeros_like(c_scr)
        n_scr[...] = jnp.zeros_like(n_scr)
        m_scr[...] = jnp.zeros_like(m_scr)
        tail_scr[...] = jnp.zeros_like(tail_scr)

    x = qk_ref[...].astype(F32)
    xe = jnp.concatenate([tail_scr[...], x], axis=0)
    cw = cw_ref[...]
    y = x * cw[MLSTM_CONV - 1:MLSTM_CONV, :]
    for k in range(1, MLSTM_CONV):
        shifted = pltpu.roll(xe, k, 0)[SUBLANES:, :]
        y = y + shifted * cw[MLSTM_CONV - 1 - k:MLSTM_CONV - k, :]
    tail_scr[...] = x[L - SUBLANES:, :]
    y = y * jax.nn.sigmoid(y)
    q_all = y[:, :MLSTM_WIDTH]
    k_all = y[:, MLSTM_WIDTH:] * (dh ** -0.5)

    sm = sm_ref[...]
    smt = smt_ref[...]
    r = lax.broadcasted_iota(jnp.int32, (L, L), 0)
    c = lax.broadcasted_iota(jnp.int32, (L, L), 1)
    causal = r >= c
    bcol_all = _dot_f32(causal.astype(F32), sm)
    brow_all = _dot_f32(smt, (r <= c).astype(F32))

    for h in range(MLSTM_HEADS):
        hs = slice(h * dh, (h + 1) * dh)
        bq = bcol_all[:, SM_MF + h:SM_MF + h + 1]
        li_c = sm[:, SM_MI + h:SM_MI + h + 1]
        br = brow_all[SM_MF + h:SM_MF + h + 1, :]
        li_r = smt[SM_MI + h:SM_MI + h + 1, :]
        m_prev = m_scr[h:h + 1, 0:1]
        b_last = bq[L - 1:L, :]

        d = jnp.where(causal, bq - br + li_r, NEG_INF)
        inter = bq + m_prev
        m_t = jnp.maximum(inter, jnp.max(d, axis=1, keepdims=True))
        w_intra = jnp.exp(d - m_t)
        w_inter = jnp.exp(inter - m_t)

        qh = q_all[:, hs]
        qb = qh.astype(BF16)
        kh = k_all[:, hs]
        vh = v_ref[:, hs]
        qk = _dot_nt(qb, kh.astype(BF16)) * w_intra
        c_prev = c_scr[h]
        n_prev = n_scr[h:h + 1, :]
        num = _dot(qk.astype(BF16), vh) + w_inter * _dot(qb, c_prev.astype(BF16))
        den = jnp.sum(qk, axis=1, keepdims=True) + w_inter * jnp.sum(qh * n_prev, axis=1, keepdims=True)
        h_c = num / jnp.maximum(jnp.abs(den), jnp.exp(-m_t))
        y_ref[:, hs] = (og_ref[:, hs].astype(F32) * h_c).astype(y_ref.dtype)

        g_c = b_last - bq + li_c
        m_new = jnp.maximum(b_last + m_prev, jnp.max(g_c, axis=0, keepdims=True))
        decay = jnp.exp(b_last + m_prev - m_new)
        kw = kh * jnp.exp(g_c - m_new)
        c_scr[h] = decay * c_prev + _dot_tn(kw.astype(BF16), vh)
        n_scr[h:h + 1, :] = decay * n_prev + jnp.sum(kw, axis=0, keepdims=True)
        m_scr[h:h + 1, :] = jnp.broadcast_to(m_new, (1, LANES))


def _mlstm(mqk, mv, mo, sm, smt, conv_w, B, S):
    T = B * S
    L = MLSTM_CHUNK
    nc = S // L
    row = lambda w: pl.BlockSpec((L, w), lambda b, c: (b * nc + c, 0))
    return pl.pallas_call(
        _mlstm_kernel,
        out_shape=jax.ShapeDtypeStruct((T, MLSTM_WIDTH), BF16),
        grid=(B, nc),
        in_specs=[row(2 * MLSTM_WIDTH), row(MLSTM_WIDTH), row(MLSTM_WIDTH), row(LANES),
                  pl.BlockSpec((SM_ROWS, L), lambda b, c: (0, b * nc + c)),
                  pl.BlockSpec((MLSTM_CONV, 2 * MLSTM_WIDTH), lambda b, c: (0, 0))],
        out_specs=row(MLSTM_WIDTH),
        scratch_shapes=[pltpu.VMEM((MLSTM_HEADS, MLSTM_HEAD_DIM, MLSTM_HEAD_DIM), F32),
                        pltpu.VMEM((SUBLANES, MLSTM_HEAD_DIM), F32),
                        pltpu.VMEM((SUBLANES, LANES), F32),
                        pltpu.VMEM((SUBLANES, 2 * MLSTM_WIDTH), F32)],
        compiler_params=_params(("parallel", "arbitrary")),
        name="mlstm",
    )(mqk, mv, mo, sm, smt, conv_w)


def _merge_kernel(ya_ref, yb_ref, yc_ref, gt_ref, x_ref, wb_ref, wo_ref, g_ref, b_ref, wr_ref, br_ref,
                  x1_ref, xs_ref, rt_ref, cnt_ref):
    tm = x_ref.shape[0]

    merged = None
    for n, y_ref in enumerate((ya_ref, yb_ref, yc_ref)):
        pr = _dot(y_ref[...], wb_ref[n]) * gt_ref[:, n * D_MODEL:(n + 1) * D_MODEL].astype(F32)
        merged = pr if merged is None else merged + pr
    mix = _dot(merged.astype(BF16), wo_ref[...])
    x1 = _layer_norm(DEEPNORM_ALPHA * x_ref[...] + mix, g_ref[...], b_ref[...])
    x1_ref[...] = x1

    logits = _dot(x1.astype(BF16), wr_ref[...]) + br_ref[...]
    lane = lax.broadcasted_iota(jnp.int32, (tm, LANES), 1)
    big = jnp.int32(LANES)
    glog = jnp.where(lane < N_GROUPS, logits, -jnp.inf)
    gmax = jnp.max(glog, axis=1, keepdims=True)
    g_top = jnp.min(jnp.where(glog == gmax, lane, big), axis=1, keepdims=True)
    p_g = 1.0 / jnp.sum(jnp.exp(glog - gmax), axis=1, keepdims=True)
    lo = ROUTER_EXPERT_LANE0 + EXPERTS_PER_GROUP * g_top
    el = jnp.where(jnp.logical_and(lane >= lo, lane < lo + EXPERTS_PER_GROUP), logits, -jnp.inf)
    m1 = jnp.max(el, axis=1, keepdims=True)
    i1 = jnp.min(jnp.where(el == m1, lane, big), axis=1, keepdims=True)
    el2 = jnp.where(lane == i1, -jnp.inf, el)
    m2 = jnp.max(el2, axis=1, keepdims=True)
    i2 = jnp.min(jnp.where(el2 == m2, lane, big), axis=1, keepdims=True)
    ratio = jnp.exp(m2 - m1)
    gate1 = p_g / (1.0 + ratio)
    gate2 = p_g * ratio / (1.0 + ratio)

    hit1 = lane == i1
    hit2 = lane == i2
    onehot = jnp.where(jnp.logical_or(hit1, hit2), 1.0, 0.0)
    r = lax.broadcasted_iota(jnp.int32, (tm, tm), 0)
    c = lax.broadcasted_iota(jnp.int32, (tm, tm), 1)
    before = jnp.where(r > c, 1.0, 0.0).astype(BF16)
    seen = _dot(before, onehot.astype(BF16))
    count = jnp.sum(onehot, axis=0, keepdims=True)
    chunks = jnp.floor((count + (MOE_ALIGN - 1)) * (1.0 / MOE_ALIGN))
    er = lax.broadcasted_iota(jnp.int32, (LANES, LANES), 0)
    ec = lax.broadcasted_iota(jnp.int32, (LANES, LANES), 1)
    earlier = jnp.where(er < ec, 1.0, 0.0)
    start = MOE_ALIGN * _dot_f32(jnp.broadcast_to(chunks, (SUBLANES, LANES)), earlier)[0:1, :]
    where_to = start + seen
    pos1 = jnp.sum(jnp.where(hit1, where_to, 0.0), axis=1, keepdims=True)
    pos2 = jnp.sum(jnp.where(hit2, where_to, 0.0), axis=1, keepdims=True)
    cnt_ref[0] = jnp.broadcast_to(count, (SUBLANES, LANES))

    e1 = (i1 - ROUTER_EXPERT_LANE0).astype(F32)
    e2 = (i2 - ROUTER_EXPERT_LANE0).astype(F32)
    rec = jnp.zeros((tm, LANES), F32)
    for pos, val in ((RT_E1, e1), (RT_E2, e2), (RT_P1, pos1), (RT_P2, pos2)):
        rec = jnp.where(lane == pos, val, rec)
    rt_ref[...] = rec[:, :RT_WIDTH]

    tag = jnp.zeros((tm, LANES), F32)
    pieces = _split3(gate1) + _split3(gate2) + (e1,)
    for pos, val in enumerate(pieces):
        tag = jnp.where(lane == pos, val, tag)
    xa = jnp.concatenate([x1.astype(BF16), tag.astype(BF16)], axis=1)
    n_local = xs_ref.shape[0]
    col = lax.broadcasted_iota(jnp.int32, (tm, n_local), 1)
    sel = jnp.logical_or(col == pos1.astype(jnp.int32), col == pos2.astype(jnp.int32))
    xs_ref[...] = _dot_tn(jnp.where(sel, 1.0, 0.0).astype(BF16), xa).astype(BF16)


def _local_rows(tm):
    worst = TOP_K * tm + N_EXPERTS * (MOE_ALIGN - 1)
    return -(-worst // LANES) * LANES


def _merge_project_route(ya, yb, yc, gates, x, p, tm):
    T, D = x.shape
    nt = T // tm
    n_local = _local_rows(tm)
    row = lambda w: pl.BlockSpec((tm, w), lambda i: (i, 0))
    const = lambda shape: _resident(shape, lambda i: tuple(0 for _ in shape))
    return pl.pallas_call(
        _merge_kernel,
        out_shape=(jax.ShapeDtypeStruct((T, D), F32),
                   jax.ShapeDtypeStruct((nt * n_local, XS_WIDTH), BF16),
                   jax.ShapeDtypeStruct((T, RT_WIDTH), F32),
                   jax.ShapeDtypeStruct((nt, SUBLANES, LANES), F32)),
        grid=(T // tm,),
        in_specs=[row(FOX_WIDTH), row(MLSTM_WIDTH), row(SGU_WIDTH), row(N_BRANCH * D), row(D),
                  const((N_BRANCH, FOX_WIDTH, D)), const((D, D)), const((1, D)), const((1, D)),
                  const((D, LANES)), const((1, LANES))],
        out_specs=(row(D), pl.BlockSpec((n_local, XS_WIDTH), lambda i: (i, 0)), row(RT_WIDTH),
                   pl.BlockSpec((1, SUBLANES, LANES), lambda i: (i, 0, 0))),
        compiler_params=_params(("parallel",)),
        name="merge_route",
    )(ya, yb, yc, gates, x, p["wb"], p["wo"], p["ln1_g"], p["ln1_b"], p["wr"], p["br"])


def _chunk_gather(src_hbm, idx_ref, dst, sem, n_chunks):
    for j in range(n_chunks):
        start = pl.multiple_of(idx_ref[0, 0, j] * MOE_ALIGN, MOE_ALIGN)
        pltpu.make_async_copy(src_hbm.at[pl.ds(start, MOE_ALIGN)], dst.at[pl.ds(j * MOE_ALIGN, MOE_ALIGN)],
                              sem).start(priority=j % 2)


def _chunk_gather_wait(src_hbm, dst, sem):
    n = dst.shape[0]
    pltpu.make_async_copy(src_hbm.at[pl.ds(0, n)], dst, sem).wait()


def _expert_kernel(be_ref, nu_ref, src0_ref, src1_ref, src2_ref, xs_hbm, wg_ref, wu_ref, wd_ref, y_ref,
                   xbuf, wg_scr, wu_scr, wd_scr, gsem):
    i = pl.program_id(0)
    n_used = nu_ref[0]
    slot = lax.rem(i, EXPERT_BUFFERS)
    active = i < n_used

    for ahead, idx_ref in ((0, src0_ref), (1, src1_ref)):
        @pl.when(jnp.logical_and(i == 0, ahead < n_used))
        def _(ahead=ahead, idx_ref=idx_ref):
            _chunk_gather(xs_hbm, idx_ref, xbuf.at[ahead], gsem.at[ahead], MOE_CHUNKS)

    @pl.when(i + 2 < n_used)
    def _():
        nxt = lax.rem(i + 2, EXPERT_BUFFERS)
        _chunk_gather(xs_hbm, src2_ref, xbuf.at[nxt], gsem.at[nxt], MOE_CHUNKS)

    new_expert = jnp.logical_or(i == 0, be_ref[i] != be_ref[jnp.maximum(i - 1, 0)])

    @pl.when(jnp.logical_and(active, new_expert))
    def _():
        wg_scr[...] = wg_ref[...].astype(BF16)
        wu_scr[...] = wu_ref[...].astype(BF16)
        wd_scr[...] = wd_ref[...].astype(BF16)

    @pl.when(active)
    def _():
        _chunk_gather_wait(xs_hbm, xbuf.at[slot], gsem.at[slot])
        xa = xbuf[slot]
        xb = xa[:, :D_MODEL]
        tag = xa[:, D_MODEL:].astype(F32)
        lane = lax.broadcasted_iota(jnp.int32, tag.shape, 1)
        pick = lambda lo, hi: jnp.sum(jnp.where(jnp.logical_and(lane >= lo, lane < hi), tag, 0.0),
                                      axis=1, keepdims=True)
        gate1 = pick(XS_G1, XS_G1 + N_SPLIT)
        gate2 = pick(XS_G2, XS_G2 + N_SPLIT)
        first = pick(XS_E1, XS_E1 + 1) == be_ref[i].astype(F32)
        g = _dot(xb, wg_scr[...])
        up = _dot(xb, wu_scr[...])
        hid = (g * jax.nn.sigmoid(g) * up).astype(BF16)
        y_ref[...] = (jnp.where(first, gate1, gate2) * _dot(hid, wd_scr[...])).astype(y_ref.dtype)

    @pl.when(jnp.logical_not(active))
    def _():
        y_ref[...] = jnp.zeros_like(y_ref)


def _expert_mlp(xs, chunk_src, block_expert, n_used, w_gate, w_up, w_down, layer):
    nb = chunk_src.shape[0]
    R = MOE_ROWS
    wsel = lambda i, be, nu: (layer, be[i], 0, 0)
    idx_spec = lambda f: pl.BlockSpec((1, 1, MOE_CHUNKS), f, memory_space=pltpu.SMEM)
    return pl.pallas_call(
        _expert_kernel,
        out_shape=jax.ShapeDtypeStruct((nb * R, D_MODEL), BF16),
        grid_spec=pltpu.PrefetchScalarGridSpec(
            num_scalar_prefetch=2,
            grid=(nb,),
            in_specs=[idx_spec(lambda i, be, nu: (i, 0, 0)),
                      idx_spec(lambda i, be, nu: (jnp.minimum(i + 1, nb - 1), 0, 0)),
                      idx_spec(lambda i, be, nu: (jnp.minimum(i + 2, nb - 1), 0, 0)),
                      pl.BlockSpec(memory_space=pl.ANY),
                      pl.BlockSpec((None, None, D_MODEL, D_EXPERT), wsel),
                      pl.BlockSpec((None, None, D_MODEL, D_EXPERT), wsel),
                      pl.BlockSpec((None, None, D_EXPERT, D_MODEL), wsel)],
            out_specs=pl.BlockSpec((R, D_MODEL), lambda i, be, nu: (i, 0)),
            scratch_shapes=[pltpu.VMEM((EXPERT_BUFFERS, R, XS_WIDTH), BF16),
                            pltpu.VMEM((D_MODEL, D_EXPERT), BF16),
                            pltpu.VMEM((D_MODEL, D_EXPERT), BF16),
                            pltpu.VMEM((D_EXPERT, D_MODEL), BF16),
                            pltpu.SemaphoreType.DMA((EXPERT_BUFFERS,))]),
        compiler_params=_params(("arbitrary",)),
        name="moe_experts",
    )(block_expert, n_used, chunk_src, chunk_src, chunk_src, xs, w_gate, w_up, w_down)


def _combine_kernel(src0_ref, src_next_ref, ys_hbm, rt_ref, x1_ref, g_ref, b_ref, o_ref, ybuf, sem):
    i = pl.program_id(0)
    nt = pl.num_programs(0)
    slot = lax.rem(i, 2)
    th = x1_ref.shape[0]
    n_local = ybuf.shape[1]
    n_chunks = n_local // MOE_ALIGN

    @pl.when(i == 0)
    def _():
        _chunk_gather(ys_hbm, src0_ref, ybuf.at[0], sem.at[0], n_chunks)

    @pl.when(i + 1 < nt)
    def _():
        _chunk_gather(ys_hbm, src_next_ref, ybuf.at[1 - slot], sem.at[1 - slot], n_chunks)

    _chunk_gather_wait(ys_hbm, ybuf.at[slot], sem.at[slot])
    rt = rt_ref[...]
    pos1 = rt[:, RT_P1:RT_P1 + 1].astype(jnp.int32)
    pos2 = rt[:, RT_P2:RT_P2 + 1].astype(jnp.int32)
    col = lax.broadcasted_iota(jnp.int32, (th, n_local), 1)
    two_hot = jnp.where(jnp.logical_or(col == pos1, col == pos2), 1.0, 0.0).astype(BF16)
    ffn = _dot(two_hot, ybuf[slot])
    o_ref[...] = _layer_norm(DEEPNORM_ALPHA * x1_ref[...] + ffn, g_ref[...], b_ref[...])


def _combine(ys, chunk_src, route, x1, g, b, th):
    T, D = x1.shape
    nt = T // th
    n_chunks = chunk_src.shape[-1]
    idx_spec = lambda f: pl.BlockSpec((1, 1, n_chunks), f, memory_space=pltpu.SMEM)
    return pl.pallas_call(
        _combine_kernel,
        out_shape=jax.ShapeDtypeStruct((T, D), F32),
        grid=(nt,),
        in_specs=[idx_spec(lambda i: (0, 0, 0)),
                  idx_spec(lambda i: (jnp.minimum(i + 1, nt - 1), 0, 0)),
                  pl.BlockSpec(memory_space=pl.ANY),
                  pl.BlockSpec((th, RT_WIDTH), lambda i: (i, 0)),
                  pl.BlockSpec((th, D), lambda i: (i, 0)),
                  pl.BlockSpec((1, D), lambda i: (0, 0)),
                  pl.BlockSpec((1, D), lambda i: (0, 0))],
        out_specs=pl.BlockSpec((th, D), lambda i: (i, 0)),
        scratch_shapes=[pltpu.VMEM((2, n_chunks * MOE_ALIGN, D), BF16), pltpu.SemaphoreType.DMA((2,))],
        compiler_params=_params(("arbitrary",)),
        name="moe_combine",
    )(chunk_src, chunk_src, ys, route, x1, g, b)


def _moe_index_kernel(ch_ref, src_ref, back_ref, be_ref, nu_ref, fill_scr, *, nt, lc, nb):
    i32 = jnp.int32

    def fill(ref, n, val):
        def body(k, carry):
            ref[k] = val
            return carry
        lax.fori_loop(0, n, body, 0, unroll=16)

    fill(src_ref, nb * MOE_CHUNKS, i32(0))
    fill(back_ref, nt * lc, i32(0))
    fill(fill_scr, nt, i32(0))

    def per_expert(e, first_block):
        pos0 = first_block * MOE_CHUNKS

        def per_tile(t, pos):
            n = ch_ref[t * N_EXPERTS + e]
            used = fill_scr[t]
            base = t * lc + used

            def per_chunk(o, carry):
                src_ref[pos + o] = base + o
                back_ref[base + o] = pos + o
                return carry

            lax.fori_loop(0, n, per_chunk, 0)
            fill_scr[t] = used + n
            return pos + n

        pos1 = lax.fori_loop(0, nt, per_tile, pos0)
        n_blocks = (pos1 - pos0 + (MOE_CHUNKS - 1)) // MOE_CHUNKS

        def mark(b, carry):
            be_ref[first_block + b] = e
            return carry

        lax.fori_loop(0, n_blocks, mark, 0)
        return first_block + n_blocks

    n_used = lax.fori_loop(0, N_EXPERTS, per_expert, i32(0))

    def tail(b, carry):
        be_ref[b] = i32(N_EXPERTS - 1)
        return carry

    lax.fori_loop(n_used, nb, tail, 0)
    nu_ref[0] = n_used


def _moe(x1, xs, route, counts, p, layer, tm):
    T = x1.shape[0]
    nt = T // tm
    lc = _local_rows(tm) // MOE_ALIGN
    nb = nt * lc // MOE_CHUNKS + N_EXPERTS
    i32 = jnp.int32
    cnt = counts[:, 0, ROUTER_EXPERT_LANE0:ROUTER_EXPERT_LANE0 + N_EXPERTS].astype(i32)
    ch = ((cnt + MOE_ALIGN - 1) // MOE_ALIGN).reshape(-1)
    smem = pl.BlockSpec(memory_space=pltpu.SMEM)
    src, back, block_expert, n_used = pl.pallas_call(
        functools.partial(_moe_index_kernel, nt=nt, lc=lc, nb=nb),
        out_shape=(jax.ShapeDtypeStruct((nb * MOE_CHUNKS,), i32), jax.ShapeDtypeStruct((nt * lc,), i32),
                   jax.ShapeDtypeStruct((nb,), i32), jax.ShapeDtypeStruct((1,), i32)),
        in_specs=[smem],
        out_specs=(smem, smem, smem, smem),
        scratch_shapes=[pltpu.SMEM((nt,), i32)],
        name="moe_index",
    )(ch)
    ys = _expert_mlp(xs, src.reshape(nb, 1, MOE_CHUNKS), block_expert, n_used,
                     p["w_gate"], p["w_up"], p["w_down"], layer)
    return _combine(ys, back.reshape(nt, 1, lc), route, x1, p["ln2_g"], p["ln2_b"], tm)


def _layer_params(l, w_in, b_in, conv_w, sgu_ln_g, sgu_ln_b, sgu_w, sgu_b, w_branch, w_out, ln1_g, ln1_b,
                  w_group, b_group, w_router, b_router, w_gate, w_up, w_down, ln2_g, ln2_b, tm):
    offs = [0]
    for s in IN_SIZES:
        offs.append(offs[-1] + s)
    wt, b = w_in[l].T, b_in[l]
    seg = lambda a, i: a[offs[i]:offs[i + 1]]
    fox_wt, fox_b = seg(wt, 0), seg(b, 0)
    fq_wt, fk_wt, fv_wt = (fox_wt[j * FOX_WIDTH:(j + 1) * FOX_WIDTH] for j in range(3))
    fq_b, fk_b, fv_b = (fox_b[j * FOX_WIDTH:(j + 1) * FOX_WIDTH] for j in range(3))
    wmt = jnp.concatenate([fk_wt, seg(wt, 2), seg(wt, 3), seg(wt, 4), seg(wt, 6), seg(wt, 7)], axis=0)
    bm = jnp.concatenate([fk_b, seg(b, 2), seg(b, 3), seg(b, 4), seg(b, 6), seg(b, 7)])
    n_small = IN_SIZES[1] + IN_SIZES[5]
    wst = jnp.concatenate([seg(wt, 1), seg(wt, 5)], axis=0).astype(BF16)
    ws = jnp.concatenate([wst, jnp.zeros((LANES - n_small, D_MODEL), BF16)], axis=0).T
    bs = jnp.concatenate([seg(b, 1), seg(b, 5), jnp.zeros((LANES - n_small,), F32)])
    reps = SGU_BLOCK // SGU_CHUNK
    sgu_bias = jnp.broadcast_to(sgu_b[l].T[:, :, None], (SGU_CHUNK, SGU_GROUPS, SGU_WIDTH // SGU_GROUPS))
    sgu_bias = jnp.tile(sgu_bias.reshape(SGU_CHUNK, SGU_WIDTH), (tm // SGU_CHUNK, 1))
    n_route = N_GROUPS + N_EXPERTS
    wr = jnp.concatenate([w_group[l], w_router[l], jnp.zeros((D_MODEL, LANES - n_route), F32)], axis=1)
    br = jnp.concatenate([b_group[l], b_router[l], jnp.zeros((LANES - n_route,), F32)])
    return {
        "wm": wmt.astype(BF16).T, "bm": bm.reshape(1, -1),
        "wqvt": jnp.concatenate([fq_wt, fv_wt], axis=0).astype(BF16),
        "bqvt": jnp.concatenate([fq_b, fv_b]).reshape(-1, 1),
        "ws": ws, "bs": bs.reshape(1, -1),
        "wst": wst, "bst": bs[:SM_ROWS].reshape(-1, 1),
        "conv_w": conv_w[l],
        "sgu_g": sgu_ln_g[l].reshape(1, -1), "sgu_b": sgu_ln_b[l].reshape(1, -1),
        "sgu_wt": jnp.tile(sgu_w[l], (1, reps, reps)), "sgu_bias": sgu_bias,
        "wb": w_branch[l].astype(BF16), "wo": w_out[l].astype(BF16),
        "ln1_g": ln1_g[l].reshape(1, -1), "ln1_b": ln1_b[l].reshape(1, -1),
        "wr": wr.astype(BF16), "br": br.reshape(1, -1),
        "w_gate": w_gate, "w_up": w_up, "w_down": w_down,
        "ln2_g": ln2_g[l].reshape(1, -1), "ln2_b": ln2_b[l].reshape(1, -1),
    }


def _tiles(B, S):
    T = B * S
    tm = 512 if T % 512 == 0 else 256
    tq = 512 if S % 512 == 0 else 256
    return tm, tq


def kernel(x, ln_in_g, ln_in_b, w_in, b_in, conv_w, sgu_ln_g, sgu_ln_b, sgu_w, sgu_b, w_branch, w_out,
           ln1_g, ln1_b, w_group, b_group, w_router, b_router, w_gate, w_up, w_down, ln2_g, ln2_b):
    B, S, D = x.shape
    assert D == D_MODEL and S % SGU_BLOCK == 0
    T = B * S
    tm, tq = _tiles(B, S)
    h = _input_layer_norm(x.reshape(T, D), ln_in_g, ln_in_b, tm)
    for l in range(DEPTH):
        p = _layer_params(l, w_in, b_in, conv_w, sgu_ln_g, sgu_ln_b, sgu_w, sgu_b, w_branch, w_out, ln1_g,
                          ln1_b, w_group, b_group, w_router, b_router, w_gate, w_up, w_down, ln2_g, ln2_b, tm)
        qt, fk, vt, mqk, mv, mo, yc, gates, sm, smt = _in_projection(h, p, tm)
        fcol, frow = _forget_cumsum(sm, smt, B, S)
        ya = _fox_attention(qt, fk, vt, fcol, frow, B, S, tq)
        yb = _mlstm(mqk, mv, mo, sm, smt, p["conv_w"], B, S)
        x1, xs, route, counts = _merge_project_route(ya, yb, yc, gates, h, p, tm)
        h = _moe(x1, xs, route, counts, p, l, tm)
    return h.reshape(B, S, D)
```

```python
import functools

import jax
import jax.numpy as jnp
from jax import lax
from jax.experimental import pallas as pl
from jax.experimental.pallas import tpu as pltpu

F32 = jnp.float32
BF16 = jnp.bfloat16

D_MODEL = 1024
DEPTH = 2
FOX_HEADS = 8
FOX_HEAD_DIM = 64
FOX_WIDTH = FOX_HEADS * FOX_HEAD_DIM
MLSTM_HEADS = 4
MLSTM_HEAD_DIM = 128
MLSTM_WIDTH = MLSTM_HEADS * MLSTM_HEAD_DIM
MLSTM_CHUNK = 128
MLSTM_CONV = 4
SGU_GROUPS = 4
SGU_WIDTH = 512
SGU_CHUNK = 128
N_BRANCH = 3
N_GROUPS = 4
EXPERTS_PER_GROUP = 8
N_EXPERTS = N_GROUPS * EXPERTS_PER_GROUP
TOP_K = 2
D_EXPERT = 512
DEEPNORM_ALPHA = (2 * DEPTH) ** 0.25
LN_EPS = 1e-5
NEG_INF = -1e30

IN_SIZES = (3 * FOX_WIDTH, FOX_HEADS, 2 * MLSTM_WIDTH, MLSTM_WIDTH, MLSTM_WIDTH,
            2 * MLSTM_HEADS, 2 * SGU_WIDTH, N_BRANCH * D_MODEL)

LANES = 128
SUBLANES = 8
VMEM_LIMIT_BYTES = 56 * 1024 * 1024

SM_FOXF = 0
SM_MI = 8
SM_MF = 12
SM_ROWS = 16

RT_E1, RT_E2, RT_P1, RT_P2 = 0, 1, 2, 3
RT_WIDTH = 8
ROUTER_EXPERT_LANE0 = N_GROUPS

MOE_ROWS = 256
MOE_ALIGN = 16
MOE_CHUNKS = MOE_ROWS // MOE_ALIGN
EXPERT_BUFFERS = 3
XS_G1, XS_G2, XS_E1 = 0, 3, 6
XS_WIDTH = D_MODEL + LANES
SGU_BLOCK = 256
CUMSUM_BLOCK = 256


def _params(sem, vmem=VMEM_LIMIT_BYTES):
    return pltpu.CompilerParams(dimension_semantics=sem, vmem_limit_bytes=vmem)


def _resident(shape, index_map):
    return pl.BlockSpec(shape, index_map, pipeline_mode=pl.Buffered(1))


def _layer_norm(x, g, b):
    mu = jnp.mean(x, axis=-1, keepdims=True)
    xc = x - mu
    var = jnp.mean(xc * xc, axis=-1, keepdims=True)
    return xc * lax.rsqrt(var + LN_EPS) * g + b


def _log_sigmoid(x):
    return jnp.minimum(x, 0.0) - jnp.log1p(jnp.exp(-jnp.abs(x)))


def _gelu_tanh(x):
    return 0.5 * x * (1.0 + jnp.tanh(0.7978845608028654 * (x + 0.044715 * x * x * x)))


def _dot(a, b):
    return jnp.dot(a, b, preferred_element_type=F32)


def _dot_nt(a, b):
    return lax.dot_general(a, b, (((1,), (1,)), ((), ())), preferred_element_type=F32)


def _dot_tn(a, b):
    return lax.dot_general(a, b, (((0,), (0,)), ((), ())), preferred_element_type=F32)


def _dot_f32(a, b):
    return jnp.dot(a, b, preferred_element_type=F32, precision=lax.Precision.HIGHEST)


def _ln_kernel(x_ref, g_ref, b_ref, o_ref):
    o_ref[...] = _layer_norm(x_ref[...], g_ref[...], b_ref[...])


def _input_layer_norm(x, g, b, tm):
    T, D = x.shape
    return pl.pallas_call(
        _ln_kernel,
        out_shape=jax.ShapeDtypeStruct((T, D), F32),
        grid=(T // tm,),
        in_specs=[pl.BlockSpec((tm, D), lambda i: (i, 0)),
                  pl.BlockSpec((1, D), lambda i: (0, 0)),
                  pl.BlockSpec((1, D), lambda i: (0, 0))],
        out_specs=pl.BlockSpec((tm, D), lambda i: (i, 0)),
        compiler_params=_params(("parallel",)),
        name="ln_in",
    )(x, g.reshape(1, D), b.reshape(1, D))


C_FK = 0
C_MQK = C_FK + FOX_WIDTH
C_MV = C_MQK + 2 * MLSTM_WIDTH
C_MO = C_MV + MLSTM_WIDTH
C_SU = C_MO + MLSTM_WIDTH
C_SV = C_SU + SGU_WIDTH
C_GATE = C_SV + SGU_WIDTH
C_END = C_GATE + N_BRANCH * D_MODEL
PROJ_PIECE = 512


def _inproj_kernel(x_ref, wm_ref, bm_ref, wqvt_ref, bqvt_ref, ws_ref, bs_ref, wst_ref, bst_ref,
                   sg_ref, sb_ref, swt_ref, sbias_ref,
                   qt_ref, fk_ref, vt_ref, mqk_ref, mv_ref, mo_ref, yc_ref, gt_ref, sm_ref, smt_ref,
                   wbd_scr):
    tm = x_ref.shape[0]

    @pl.when(pl.program_id(0) == 0)
    def _():
        r = lax.broadcasted_iota(jnp.int32, (SGU_BLOCK, SGU_BLOCK), 0)
        c = lax.broadcasted_iota(jnp.int32, (SGU_BLOCK, SGU_BLOCK), 1)
        keep = jnp.logical_and(r // SGU_CHUNK == c // SGU_CHUNK, r >= c)
        for g in range(SGU_GROUPS):
            wbd_scr[g] = jnp.where(keep, swt_ref[g], 0.0).astype(BF16)

    xb = x_ref[...].astype(BF16)

    def proj(lo, width=PROJ_PIECE):
        return _dot(xb, wm_ref[:, lo:lo + width]) + bm_ref[:, lo:lo + width]

    fk_ref[...] = proj(C_FK).astype(BF16)
    qvt = _dot_nt(wqvt_ref[...], xb) + bqvt_ref[...]
    qt_ref[...] = (qvt[:FOX_WIDTH] * (LOG2E * FOX_HEAD_DIM ** -0.5)).astype(BF16)
    vt_ref[...] = qvt[FOX_WIDTH:].astype(BF16)
    for j in range(2):
        mqk_ref[:, j * PROJ_PIECE:(j + 1) * PROJ_PIECE] = proj(C_MQK + j * PROJ_PIECE).astype(BF16)
    mv_ref[...] = proj(C_MV).astype(BF16)
    mo_ref[...] = jax.nn.sigmoid(proj(C_MO)).astype(BF16)
    for j in range(N_BRANCH * D_MODEL // PROJ_PIECE):
        gt_ref[:, j * PROJ_PIECE:(j + 1) * PROJ_PIECE] = jax.nn.sigmoid(
            proj(C_GATE + j * PROJ_PIECE)).astype(BF16)

    sm = _dot(xb, ws_ref[...]) + bs_ref[...]
    lane = lax.broadcasted_iota(jnp.int32, sm.shape, 1)
    is_ls = jnp.logical_or(lane < SM_MI, jnp.logical_and(lane >= SM_MF, lane < SM_ROWS))
    sm_ref[...] = jnp.where(is_ls, _log_sigmoid(sm), sm)
    smt = _dot_nt(wst_ref[...], xb) + bst_ref[...]
    row = lax.broadcasted_iota(jnp.int32, smt.shape, 0)
    is_ls_t = jnp.logical_or(row < SM_MI, row >= SM_MF)
    smt_ref[...] = jnp.where(is_ls_t, _log_sigmoid(smt), smt)

    u = _gelu_tanh(proj(C_SU))
    v = _gelu_tanh(proj(C_SV))
    vn = _layer_norm(v, sg_ref[...], sb_ref[...]).astype(BF16)
    for rb in range(tm // SGU_BLOCK):
        rows = slice(rb * SGU_BLOCK, (rb + 1) * SGU_BLOCK)
        for g in range(SGU_GROUPS):
            cols = slice(g * LANES, (g + 1) * LANES)
            z = _dot(wbd_scr[g], vn[rows, cols]) + sbias_ref[rows, cols]
            yc_ref[rows, cols] = (u[rows, cols] * z).astype(BF16)


def _in_projection(x, p, tm):
    T, D = x.shape
    row = lambda w: pl.BlockSpec((tm, w), lambda i: (i, 0))
    out_shapes = (
        jax.ShapeDtypeStruct((FOX_WIDTH, T), BF16),
        jax.ShapeDtypeStruct((T, FOX_WIDTH), BF16),
        jax.ShapeDtypeStruct((FOX_WIDTH, T), BF16),
        jax.ShapeDtypeStruct((T, 2 * MLSTM_WIDTH), BF16),
        jax.ShapeDtypeStruct((T, MLSTM_WIDTH), BF16),
        jax.ShapeDtypeStruct((T, MLSTM_WIDTH), BF16),
        jax.ShapeDtypeStruct((T, SGU_WIDTH), BF16),
        jax.ShapeDtypeStruct((T, N_BRANCH * D_MODEL), BF16),
        jax.ShapeDtypeStruct((T, LANES), F32),
        jax.ShapeDtypeStruct((SM_ROWS, T), F32),
    )
    col = lambda h: pl.BlockSpec((h, tm), lambda i: (0, i))
    out_specs = (col(FOX_WIDTH), row(FOX_WIDTH), col(FOX_WIDTH),
                 row(2 * MLSTM_WIDTH), row(MLSTM_WIDTH), row(MLSTM_WIDTH), row(SGU_WIDTH),
                 row(N_BRANCH * D_MODEL), row(LANES), col(SM_ROWS))
    const = lambda shape: _resident(shape, lambda i: tuple(0 for _ in shape))
    in_specs = [
        row(D),
        const((D, C_END)), const((1, C_END)),
        const((2 * FOX_WIDTH, D)), const((2 * FOX_WIDTH, 1)),
        const((D, LANES)), const((1, LANES)),
        const((SM_ROWS, D)), const((SM_ROWS, 1)),
        const((1, SGU_WIDTH)), const((1, SGU_WIDTH)),
        const((SGU_GROUPS, SGU_BLOCK, SGU_BLOCK)),
        const((tm, SGU_WIDTH)),
    ]
    return pl.pallas_call(
        _inproj_kernel,
        out_shape=out_shapes,
        grid=(T // tm,),
        in_specs=in_specs,
        out_specs=out_specs,
        scratch_shapes=[pltpu.VMEM((SGU_GROUPS, SGU_BLOCK, SGU_BLOCK), BF16)],
        compiler_params=_params(("arbitrary",)),
        name="in_proj",
    )(x, p["wm"], p["bm"], p["wqvt"], p["bqvt"], p["ws"], p["bs"], p["wst"], p["bst"],
      p["sgu_g"], p["sgu_b"], p["sgu_wt"], p["sgu_bias"])


def _fcum_kernel(sm_ref, smt_ref, fcol_ref, frow_ref):
    S = sm_ref.shape[0]
    cb = min(CUMSUM_BLOCK, S)
    r = lax.broadcasted_iota(jnp.int32, (cb, cb), 0)
    c = lax.broadcasted_iota(jnp.int32, (cb, cb), 1)
    lower = (r >= c).astype(F32)
    upper = (r <= c).astype(F32)
    carry_c = jnp.zeros((1, LANES), F32)
    carry_r = jnp.zeros((SM_ROWS, 1), F32)
    for blk in range(S // cb):
        sl = slice(blk * cb, (blk + 1) * cb)
        fc = _dot_f32(lower, sm_ref[sl, :]) + carry_c
        fcol_ref[sl, :] = fc
        carry_c = fc[cb - 1:cb, :]
        fr = _dot_f32(smt_ref[:, sl], upper) + carry_r
        frow_ref[:, sl] = fr
        carry_r = fr[:, cb - 1:cb]


def _forget_cumsum(sm, smt, B, S):
    T = B * S
    return pl.pallas_call(
        _fcum_kernel,
        out_shape=(jax.ShapeDtypeStruct((T, LANES), F32), jax.ShapeDtypeStruct((SM_ROWS, T), F32)),
        grid=(B,),
        in_specs=[pl.BlockSpec((S, LANES), lambda b: (b, 0)),
                  pl.BlockSpec((SM_ROWS, S), lambda b: (0, b))],
        out_specs=(pl.BlockSpec((S, LANES), lambda b: (b, 0)),
                   pl.BlockSpec((SM_ROWS, S), lambda b: (0, b))),
        compiler_params=_params(("parallel",)),
        name="forget_cumsum",
    )(sm, smt)


N_SPLIT = 3
LOG2E = 1.4426950408889634


def _split3(f):
    hi = f.astype(BF16).astype(F32)
    r1 = f - hi
    mid = r1.astype(BF16).astype(F32)
    lo = r1 - mid
    return hi, mid, lo


def _fox_kernel(qt_ref, k_ref, vt_ref, fcol_ref, frow_ref, o_ref, kaug_scr, s_scr, m_scr, l_scr, acc_scr, *, tq):
    S = k_ref.shape[0]
    nq = S // tq
    pair = pl.program_id(1)

    lane = lax.broadcasted_iota(jnp.int32, (S, LANES), 1)
    fc = fcol_ref[...]
    k = k_ref[...]
    for hh in range(2):
        f = jnp.sum(jnp.where(lane == 2 * pair + hh, fc, 0.0), axis=1, keepdims=True)
        hi, mid, lo = _split3(LOG2E * f)
        aug = jnp.where(lane < N_SPLIT, 1.0,
                        jnp.where(lane == N_SPLIT, -hi,
                                  jnp.where(lane == N_SPLIT + 1, -mid,
                                            jnp.where(lane == N_SPLIT + 2, -lo, 0.0))))
        kaug_scr[hh, :, 0:LANES] = k
        kaug_scr[hh, :, LANES:2 * LANES] = aug.astype(BF16)

    sub = lax.broadcasted_iota(jnp.int32, (LANES, tq), 0)
    key = lax.broadcasted_iota(jnp.int32, (tq, tq), 0)
    qry = lax.broadcasted_iota(jnp.int32, (tq, tq), 1)
    causal = key <= qry

    def start_tile(qi):
        cols = slice(qi * tq, (qi + 1) * tq)
        qt = qt_ref[:, cols]
        rhs = []
        for hh in range(2):
            hi, mid, lo = _split3(LOG2E * frow_ref[hh][:, cols])
            augq = jnp.where(sub == 0, hi,
                             jnp.where(sub == 1, mid,
                                       jnp.where(sub == 2, lo,
                                                 jnp.where(sub < 2 * N_SPLIT, 1.0, 0.0))))
            in_head = (sub < FOX_HEAD_DIM) if hh == 0 else (sub >= FOX_HEAD_DIM)
            qm = jnp.where(in_head, qt, jnp.zeros_like(qt))
            rhs.append(jnp.concatenate([qm, augq.astype(BF16)], axis=0))
            m_scr[qi % 2, hh] = jnp.full((1, tq), NEG_INF, F32)
            l_scr[qi % 2, hh] = jnp.zeros((1, tq), F32)
            acc_scr[qi % 2, hh] = jnp.zeros((LANES, tq), F32)
        return rhs

    def scores(rhs, j, hh, slot):
        s_scr[slot, hh] = _dot(kaug_scr[hh, j * tq:(j + 1) * tq, :], rhs[hh])

    def consume(qi, j, hh, slot):
        par = qi % 2
        s = s_scr[slot, hh]
        if j == qi:
            s = jnp.where(causal, s, NEG_INF)
        m = m_scr[par, hh]
        m_new = jnp.maximum(m, jnp.max(s, axis=0, keepdims=True))
        p = jnp.exp2(s - m_new)
        corr = jnp.exp2(m - m_new)
        m_scr[par, hh] = m_new
        l_scr[par, hh] = l_scr[par, hh] * corr + jnp.sum(p, axis=0, keepdims=True)
        acc_scr[par, hh] = acc_scr[par, hh] * corr + _dot(vt_ref[:, j * tq:(j + 1) * tq], p.astype(BF16))

    steps = [(qi, j) for qi in range(nq) for j in range(qi + 1)]
    rhs = start_tile(0)
    for hh in range(2):
        scores(rhs, 0, hh, 0)
    for t, (qi, j) in enumerate(steps):
        slot = t % 2
        nxt = steps[t + 1] if t + 1 < len(steps) else None
        rhs_next = rhs
        if nxt is not None and nxt[1] == 0:
            rhs_next = start_tile(nxt[0])
        for hh in range(2):
            if nxt is not None:
                scores(rhs_next, nxt[1], hh, 1 - slot)
            consume(qi, j, hh, slot)
        if j == qi:
            outs = [acc_scr[qi % 2, hh] / l_scr[qi % 2, hh] for hh in range(2)]
            out_t = jnp.where(sub < FOX_HEAD_DIM, outs[0], outs[1])
            o_ref[qi * tq:(qi + 1) * tq, :] = out_t.T.astype(o_ref.dtype)
        rhs = rhs_next


def _fox_attention(qt, fk, vt, fcol, frow, B, S, tq):
    T = B * S
    n_pairs = FOX_HEADS // 2
    frow3 = frow.reshape(SM_ROWS, 1, T)
    return pl.pallas_call(
        functools.partial(_fox_kernel, tq=tq),
        out_shape=jax.ShapeDtypeStruct((T, FOX_WIDTH), BF16),
        grid=(B, n_pairs),
        in_specs=[pl.BlockSpec((LANES, S), lambda b, p: (p, b)),
                  pl.BlockSpec((S, LANES), lambda b, p: (b, p)),
                  pl.BlockSpec((LANES, S), lambda b, p: (p, b)),
                  pl.BlockSpec((S, LANES), lambda b, p: (b, 0)),
                  pl.BlockSpec((2, 1, S), lambda b, p: (p, 0, b))],
        out_specs=pl.BlockSpec((S, LANES), lambda b, p: (b, p)),
        scratch_shapes=[pltpu.VMEM((2, S, 2 * LANES), BF16),
                        pltpu.VMEM((2, 2, tq, tq), F32),
                        pltpu.VMEM((2, 2, 1, tq), F32),
                        pltpu.VMEM((2, 2, 1, tq), F32),
                        pltpu.VMEM((2, 2, LANES, tq), F32)],
        compiler_params=_params(("parallel", "parallel")),
        name="fox_attention",
    )(qt, fk, vt, fcol, frow3)


def _mlstm_kernel(qk_ref, v_ref, og_ref, sm_ref, smt_ref, cw_ref, y_ref, c_scr, n_scr, m_scr, tail_scr):
    L = MLSTM_CHUNK
    dh = MLSTM_HEAD_DIM

    @pl.when(pl.program_id(1) == 0)
    def _():
        c_scr[...] = jnp.zeros_like(c_scr)
        n_scr[...] = jnp.zeros_like(n_scr)
        m_scr[...] = jnp.zeros_like(m_scr)
        tail_scr[...] = jnp.zeros_like(tail_scr)

    x = qk_ref[...].astype(F32)
    xe = jnp.concatenate([tail_scr[...], x], axis=0)
    cw = cw_ref[...]
    y = x * cw[MLSTM_CONV - 1:MLSTM_CONV, :]
    for k in range(1, MLSTM_CONV):
        shifted = pltpu.roll(xe, k, 0)[SUBLANES:, :]
        y = y + shifted * cw[MLSTM_CONV - 1 - k:MLSTM_CONV - k, :]
    tail_scr[...] = x[L - SUBLANES:, :]
    y = y * jax.nn.sigmoid(y)
    q_all = y[:, :MLSTM_WIDTH]
    k_all = y[:, MLSTM_WIDTH:] * (dh ** -0.5)

    sm = sm_ref[...]
    smt = smt_ref[...]
    r = lax.broadcasted_iota(jnp.int32, (L, L), 0)
    c = lax.broadcasted_iota(jnp.int32, (L, L), 1)
    causal = r >= c
    bcol_all = _dot_f32(causal.astype(F32), sm)
    brow_all = _dot_f32(smt, (r <= c).astype(F32))

    for h in range(MLSTM_HEADS):
        hs = slice(h * dh, (h + 1) * dh)
        bq = bcol_all[:, SM_MF + h:SM_MF + h + 1]
        li_c = sm[:, SM_MI + h:SM_MI + h + 1]
        br = brow_all[SM_MF + h:SM_MF + h + 1, :]
        li_r = smt[SM_MI + h:SM_MI + h + 1, :]
        m_prev = m_scr[h:h + 1, 0:1]
        b_last = bq[L - 1:L, :]

        d = jnp.where(causal, bq - br + li_r, NEG_INF)
        inter = bq + m_prev
        m_t = jnp.maximum(inter, jnp.max(d, axis=1, keepdims=True))
        w_intra = jnp.exp(d - m_t)
        w_inter = jnp.exp(inter - m_t)

        qh = q_all[:, hs]
        qb = qh.astype(BF16)
        kh = k_all[:, hs]
        vh = v_ref[:, hs]
        qk = _dot_nt(qb, kh.astype(BF16)) * w_intra
        c_prev = c_scr[h]
        n_prev = n_scr[h:h + 1, :]
        num = _dot(qk.astype(BF16), vh) + w_inter * _dot(qb, c_prev.astype(BF16))
        den = jnp.sum(qk, axis=1, keepdims=True) + w_inter * jnp.sum(qh * n_prev, axis=1, keepdims=True)
        h_c = num / jnp.maximum(jnp.abs(den), jnp.exp(-m_t))
        y_ref[:, hs] = (og_ref[:, hs].astype(F32) * h_c).astype(y_ref.dtype)

        g_c = b_last - bq + li_c
        m_new = jnp.maximum(b_last + m_prev, jnp.max(g_c, axis=0, keepdims=True))
        decay = jnp.exp(b_last + m_prev - m_new)
        kw = kh * jnp.exp(g_c - m_new)
        c_scr[h] = decay * c_prev + _dot_tn(kw.astype(BF16), vh)
        n_scr[h:h + 1, :] = decay * n_prev + jnp.sum(kw, axis=0, keepdims=True)
        m_scr[h:h + 1, :] = jnp.broadcast_to(m_new, (1, LANES))


def _mlstm(mqk, mv, mo, sm, smt, conv_w, B, S):
    T = B * S
    L = MLSTM_CHUNK
    nc = S // L
    row = lambda w: pl.BlockSpec((L, w), lambda b, c: (b * nc + c, 0))
    return pl.pallas_call(
        _mlstm_kernel,
        out_shape=jax.ShapeDtypeStruct((T, MLSTM_WIDTH), BF16),
        grid=(B, nc),
        in_specs=[row(2 * MLSTM_WIDTH), row(MLSTM_WIDTH), row(MLSTM_WIDTH), row(LANES),
                  pl.BlockSpec((SM_ROWS, L), lambda b, c: (0, b * nc + c)),
                  pl.BlockSpec((MLSTM_CONV, 2 * MLSTM_WIDTH), lambda b, c: (0, 0))],
        out_specs=row(MLSTM_WIDTH),
        scratch_shapes=[pltpu.VMEM((MLSTM_HEADS, MLSTM_HEAD_DIM, MLSTM_HEAD_DIM), F32),
                        pltpu.VMEM((SUBLANES, MLSTM_HEAD_DIM), F32),
                        pltpu.VMEM((SUBLANES, LANES), F32),
                        pltpu.VMEM((SUBLANES, 2 * MLSTM_WIDTH), F32)],
        compiler_params=_params(("parallel", "arbitrary")),
        name="mlstm",
    )(mqk, mv, mo, sm, smt, conv_w)


def _merge_kernel(ya_ref, yb_ref, yc_ref, gt_ref, x_ref, wb_ref, wo_ref, g_ref, b_ref, wr_ref, br_ref,
                  x1_ref, xs_ref, rt_ref, cnt_ref, x1_scr):
    tm = x_ref.shape[0]

    @pl.when(pl.program_id(0) == 0)
    def _():
        x1_scr[...] = jnp.zeros_like(x1_scr)

    xp = x1_scr[...]
    xpb = xp.astype(BF16)
    lane = lax.broadcasted_iota(jnp.int32, (tm, LANES), 1)
    big = jnp.int32(LANES)

    def branch(n, y_ref):
        return _dot(y_ref[...], wb_ref[n]) * gt_ref[:, n * D_MODEL:(n + 1) * D_MODEL].astype(F32)

    merged = branch(0, ya_ref)

    logits = _dot(xpb, wr_ref[...]) + br_ref[...]
    glog = jnp.where(lane < N_GROUPS, logits, -jnp.inf)
    gmax = jnp.max(glog, axis=1, keepdims=True)
    g_top = jnp.min(jnp.where(glog == gmax, lane, big), axis=1, keepdims=True)
    p_g = 1.0 / jnp.sum(jnp.exp(glog - gmax), axis=1, keepdims=True)

    merged = merged + branch(1, yb_ref)

    lo = ROUTER_EXPERT_LANE0 + EXPERTS_PER_GROUP * g_top
    el = jnp.where(jnp.logical_and(lane >= lo, lane < lo + EXPERTS_PER_GROUP), logits, -jnp.inf)
    m1 = jnp.max(el, axis=1, keepdims=True)
    i1 = jnp.min(jnp.where(el == m1, lane, big), axis=1, keepdims=True)
    el2 = jnp.where(lane == i1, -jnp.inf, el)
    m2 = jnp.max(el2, axis=1, keepdims=True)
    i2 = jnp.min(jnp.where(el2 == m2, lane, big), axis=1, keepdims=True)
    ratio = jnp.exp(m2 - m1)
    gate1 = p_g / (1.0 + ratio)
    gate2 = p_g * ratio / (1.0 + ratio)

    merged = merged + branch(2, yc_ref)

    hit1 = lane == i1
    hit2 = lane == i2
    onehot = jnp.where(jnp.logical_or(hit1, hit2), 1.0, 0.0)
    r = lax.broadcasted_iota(jnp.int32, (tm, tm), 0)
    c = lax.broadcasted_iota(jnp.int32, (tm, tm), 1)
    before = jnp.where(r > c, 1.0, 0.0).astype(BF16)
    seen = _dot(before, onehot.astype(BF16))
    count = jnp.sum(onehot, axis=0, keepdims=True)
    chunks = jnp.floor((count + (MOE_ALIGN - 1)) * (1.0 / MOE_ALIGN))
    er = lax.broadcasted_iota(jnp.int32, (LANES, LANES), 0)
    ec = lax.broadcasted_iota(jnp.int32, (LANES, LANES), 1)
    earlier = jnp.where(er < ec, 1.0, 0.0)
    start = MOE_ALIGN * _dot_f32(jnp.broadcast_to(chunks, (SUBLANES, LANES)), earlier)[0:1, :]
    where_to = start + seen
    pos1 = jnp.sum(jnp.where(hit1, where_to, 0.0), axis=1, keepdims=True)
    pos2 = jnp.sum(jnp.where(hit2, where_to, 0.0), axis=1, keepdims=True)
    cnt_ref[0] = jnp.broadcast_to(count, (SUBLANES, LANES))

    e1 = (i1 - ROUTER_EXPERT_LANE0).astype(F32)
    e2 = (i2 - ROUTER_EXPERT_LANE0).astype(F32)
    rec = jnp.zeros((tm, LANES), F32)
    for pos, val in ((RT_E1, e1), (RT_E2, e2), (RT_P1, pos1), (RT_P2, pos2)):
        rec = jnp.where(lane == pos, val, rec)
    rt_ref[...] = rec[:, :RT_WIDTH]

    mix = _dot(merged.astype(BF16), wo_ref[...])

    tag = jnp.zeros((tm, LANES), F32)
    pieces = _split3(gate1) + _split3(gate2) + (e1,)
    for pos, val in enumerate(pieces):
        tag = jnp.where(lane == pos, val, tag)
    xa = jnp.concatenate([xpb, tag.astype(BF16)], axis=1)
    n_local = xs_ref.shape[0]
    col = lax.broadcasted_iota(jnp.int32, (tm, n_local), 1)
    sel = jnp.logical_or(col == pos1.astype(jnp.int32), col == pos2.astype(jnp.int32))
    xs_ref[...] = _dot_tn(jnp.where(sel, 1.0, 0.0).astype(BF16), xa).astype(BF16)

    x1 = _layer_norm(DEEPNORM_ALPHA * x_ref[...] + mix, g_ref[...], b_ref[...])
    x1_ref[...] = x1
    x1_scr[...] = x1


def _local_rows(tm):
    worst = TOP_K * tm + N_EXPERTS * (MOE_ALIGN - 1)
    return -(-worst // LANES) * LANES


def _merge_project_route(ya, yb, yc, gates, x, p, tm):
    T, D = x.shape
    nt = T // tm
    n_local = _local_rows(tm)
    cur = lambda i: jnp.minimum(i, nt - 1)
    prev = lambda i: jnp.maximum(i - 1, 0)
    row = lambda w: pl.BlockSpec((tm, w), lambda i: (cur(i), 0))
    const = lambda shape: _resident(shape, lambda i: tuple(0 for _ in shape))
    return pl.pallas_call(
        _merge_kernel,
        out_shape=(jax.ShapeDtypeStruct((T, D), F32),
                   jax.ShapeDtypeStruct((nt * n_local, XS_WIDTH), BF16),
                   jax.ShapeDtypeStruct((T, RT_WIDTH), F32),
                   jax.ShapeDtypeStruct((nt, SUBLANES, LANES), F32)),
        grid=(nt + 1,),
        in_specs=[row(FOX_WIDTH), row(MLSTM_WIDTH), row(SGU_WIDTH), row(N_BRANCH * D), row(D),
                  const((N_BRANCH, FOX_WIDTH, D)), const((D, D)), const((1, D)), const((1, D)),
                  const((D, LANES)), const((1, LANES))],
        out_specs=(row(D), pl.BlockSpec((n_local, XS_WIDTH), lambda i: (prev(i), 0)),
                   pl.BlockSpec((tm, RT_WIDTH), lambda i: (prev(i), 0)),
                   pl.BlockSpec((1, SUBLANES, LANES), lambda i: (prev(i), 0, 0))),
        scratch_shapes=[pltpu.VMEM((tm, D), F32)],
        compiler_params=_params(("arbitrary",)),
        name="merge_route",
    )(ya, yb, yc, gates, x, p["wb"], p["wo"], p["ln1_g"], p["ln1_b"], p["wr"], p["br"])


def _chunk_gather(src_hbm, idx_ref, dst, sem, n_chunks):
    for j in range(n_chunks):
        start = pl.multiple_of(idx_ref[0, 0, j] * MOE_ALIGN, MOE_ALIGN)
        pltpu.make_async_copy(src_hbm.at[pl.ds(start, MOE_ALIGN)], dst.at[pl.ds(j * MOE_ALIGN, MOE_ALIGN)],
                              sem).start(priority=j % 2)


def _chunk_gather_wait(src_hbm, dst, sem):
    n = dst.shape[0]
    pltpu.make_async_copy(src_hbm.at[pl.ds(0, n)], dst, sem).wait()


def _expert_kernel(be_ref, nu_ref, src0_ref, src1_ref, src2_ref, xs_hbm, wg_ref, wu_ref, wd_ref, y_ref,
                   xbuf, wg_scr, wu_scr, wd_scr, gsem):
    i = pl.program_id(0)
    n_used = nu_ref[0]
    slot = lax.rem(i, EXPERT_BUFFERS)
    active = i < n_used

    for ahead, idx_ref in ((0, src0_ref), (1, src1_ref)):
        @pl.when(jnp.logical_and(i == 0, ahead < n_used))
        def _(ahead=ahead, idx_ref=idx_ref):
            _chunk_gather(xs_hbm, idx_ref, xbuf.at[ahead], gsem.at[ahead], MOE_CHUNKS)

    @pl.when(i + 2 < n_used)
    def _():
        nxt = lax.rem(i + 2, EXPERT_BUFFERS)
        _chunk_gather(xs_hbm, src2_ref, xbuf.at[nxt], gsem.at[nxt], MOE_CHUNKS)

    new_expert = jnp.logical_or(i == 0, be_ref[i] != be_ref[jnp.maximum(i - 1, 0)])

    @pl.when(jnp.logical_and(active, new_expert))
    def _():
        wg_scr[...] = wg_ref[...].astype(BF16)
        wu_scr[...] = wu_ref[...].astype(BF16)
        wd_scr[...] = wd_ref[...].astype(BF16)

    @pl.when(active)
    def _():
        _chunk_gather_wait(xs_hbm, xbuf.at[slot], gsem.at[slot])
        xa = xbuf[slot]
        xb = xa[:, :D_MODEL]
        tag = xa[:, D_MODEL:].astype(F32)
        lane = lax.broadcasted_iota(jnp.int32, tag.shape, 1)
        pick = lambda lo, hi: jnp.sum(jnp.where(jnp.logical_and(lane >= lo, lane < hi), tag, 0.0),
                                      axis=1, keepdims=True)
        gate1 = pick(XS_G1, XS_G1 + N_SPLIT)
        gate2 = pick(XS_G2, XS_G2 + N_SPLIT)
        first = pick(XS_E1, XS_E1 + 1) == be_ref[i].astype(F32)
        g = _dot(xb, wg_scr[...])
        up = _dot(xb, wu_scr[...])
        hid = (g * jax.nn.sigmoid(g) * up).astype(BF16)
        y_ref[...] = (jnp.where(first, gate1, gate2) * _dot(hid, wd_scr[...])).astype(y_ref.dtype)

    @pl.when(jnp.logical_not(active))
    def _():
        y_ref[...] = jnp.zeros_like(y_ref)


def _expert_mlp(xs, chunk_src, block_expert, n_used, w_gate, w_up, w_down, layer):
    nb = chunk_src.shape[0]
    R = MOE_ROWS
    wsel = lambda i, be, nu: (layer, be[i], 0, 0)
    idx_spec = lambda f: pl.BlockSpec((1, 1, MOE_CHUNKS), f, memory_space=pltpu.SMEM)
    return pl.pallas_call(
        _expert_kernel,
        out_shape=jax.ShapeDtypeStruct((nb * R, D_MODEL), BF16),
        grid_spec=pltpu.PrefetchScalarGridSpec(
            num_scalar_prefetch=2,
            grid=(nb,),
            in_specs=[idx_spec(lambda i, be, nu: (i, 0, 0)),
                      idx_spec(lambda i, be, nu: (jnp.minimum(i + 1, nb - 1), 0, 0)),
                      idx_spec(lambda i, be, nu: (jnp.minimum(i + 2, nb - 1), 0, 0)),
                      pl.BlockSpec(memory_space=pl.ANY),
                      pl.BlockSpec((None, None, D_MODEL, D_EXPERT), wsel),
                      pl.BlockSpec((None, None, D_MODEL, D_EXPERT), wsel),
                      pl.BlockSpec((None, None, D_EXPERT, D_MODEL), wsel)],
            out_specs=pl.BlockSpec((R, D_MODEL), lambda i, be, nu: (i, 0)),
            scratch_shapes=[pltpu.VMEM((EXPERT_BUFFERS, R, XS_WIDTH), BF16),
                            pltpu.VMEM((D_MODEL, D_EXPERT), BF16),
                            pltpu.VMEM((D_MODEL, D_EXPERT), BF16),
                            pltpu.VMEM((D_EXPERT, D_MODEL), BF16),
                            pltpu.SemaphoreType.DMA((EXPERT_BUFFERS,))]),
        compiler_params=_params(("arbitrary",)),
        name="moe_experts",
    )(block_expert, n_used, chunk_src, chunk_src, chunk_src, xs, w_gate, w_up, w_down)


def _combine_kernel(src0_ref, src_next_ref, ys_hbm, rt_ref, x1_ref, g_ref, b_ref, o_ref, ybuf, sem):
    i = pl.program_id(0)
    nt = pl.num_programs(0)
    slot = lax.rem(i, 2)
    th = x1_ref.shape[0]
    n_local = ybuf.shape[1]
    n_chunks = n_local // MOE_ALIGN

    @pl.when(i == 0)
    def _():
        _chunk_gather(ys_hbm, src0_ref, ybuf.at[0], sem.at[0], n_chunks)

    @pl.when(i + 1 < nt)
    def _():
        _chunk_gather(ys_hbm, src_next_ref, ybuf.at[1 - slot], sem.at[1 - slot], n_chunks)

    _chunk_gather_wait(ys_hbm, ybuf.at[slot], sem.at[slot])
    rt = rt_ref[...]
    pos1 = rt[:, RT_P1:RT_P1 + 1].astype(jnp.int32)
    pos2 = rt[:, RT_P2:RT_P2 + 1].astype(jnp.int32)
    col = lax.broadcasted_iota(jnp.int32, (th, n_local), 1)
    two_hot = jnp.where(jnp.logical_or(col == pos1, col == pos2), 1.0, 0.0).astype(BF16)
    ffn = _dot(two_hot, ybuf[slot])
    o_ref[...] = _layer_norm(DEEPNORM_ALPHA * x1_ref[...] + ffn, g_ref[...], b_ref[...])


def _combine(ys, chunk_src, route, x1, g, b, th):
    T, D = x1.shape
    nt = T // th
    n_chunks = chunk_src.shape[-1]
    idx_spec = lambda f: pl.BlockSpec((1, 1, n_chunks), f, memory_space=pltpu.SMEM)
    return pl.pallas_call(
        _combine_kernel,
        out_shape=jax.ShapeDtypeStruct((T, D), F32),
        grid=(nt,),
        in_specs=[idx_spec(lambda i: (0, 0, 0)),
                  idx_spec(lambda i: (jnp.minimum(i + 1, nt - 1), 0, 0)),
                  pl.BlockSpec(memory_space=pl.ANY),
                  pl.BlockSpec((th, RT_WIDTH), lambda i: (i, 0)),
                  pl.BlockSpec((th, D), lambda i: (i, 0)),
                  pl.BlockSpec((1, D), lambda i: (0, 0)),
                  pl.BlockSpec((1, D), lambda i: (0, 0))],
        out_specs=pl.BlockSpec((th, D), lambda i: (i, 0)),
        scratch_shapes=[pltpu.VMEM((2, n_chunks * MOE_ALIGN, D), BF16), pltpu.SemaphoreType.DMA((2,))],
        compiler_params=_params(("arbitrary",)),
        name="moe_combine",
    )(chunk_src, chunk_src, ys, route, x1, g, b)


def _moe_index_kernel(ch_ref, src_ref, back_ref, be_ref, nu_ref, fill_scr, *, nt, lc, nb):
    i32 = jnp.int32

    def fill(ref, n, val):
        def body(k, carry):
            ref[k] = val
            return carry
        lax.fori_loop(0, n, body, 0, unroll=16)

    fill(src_ref, nb * MOE_CHUNKS, i32(0))
    fill(back_ref, nt * lc, i32(0))
    fill(fill_scr, nt, i32(0))

    def per_expert(e, first_block):
        pos0 = first_block * MOE_CHUNKS

        def per_tile(t, pos):
            n = ch_ref[t * N_EXPERTS + e]
            used = fill_scr[t]
            base = t * lc + used

            def per_chunk(o, carry):
                src_ref[pos + o] = base + o
                back_ref[base + o] = pos + o
                return carry

            lax.fori_loop(0, n, per_chunk, 0)
            fill_scr[t] = used + n
            return pos + n

        pos1 = lax.fori_loop(0, nt, per_tile, pos0)
        n_blocks = (pos1 - pos0 + (MOE_CHUNKS - 1)) // MOE_CHUNKS

        def mark(b, carry):
            be_ref[first_block + b] = e
            return carry

        lax.fori_loop(0, n_blocks, mark, 0)
        return first_block + n_blocks

    n_used = lax.fori_loop(0, N_EXPERTS, per_expert, i32(0))

    def tail(b, carry):
        be_ref[b] = i32(N_EXPERTS - 1)
        return carry

    lax.fori_loop(n_used, nb, tail, 0)
    nu_ref[0] = n_used


def _moe(x1, xs, route, counts, p, layer, tm):
    T = x1.shape[0]
    nt = T // tm
    lc = _local_rows(tm) // MOE_ALIGN
    nb = nt * lc // MOE_CHUNKS + N_EXPERTS
    i32 = jnp.int32
    cnt = counts[:, 0, ROUTER_EXPERT_LANE0:ROUTER_EXPERT_LANE0 + N_EXPERTS].astype(i32)
    ch = ((cnt + MOE_ALIGN - 1) // MOE_ALIGN).reshape(-1)
    smem = pl.BlockSpec(memory_space=pltpu.SMEM)
    src, back, block_expert, n_used = pl.pallas_call(
        functools.partial(_moe_index_kernel, nt=nt, lc=lc, nb=nb),
        out_shape=(jax.ShapeDtypeStruct((nb * MOE_CHUNKS,), i32), jax.ShapeDtypeStruct((nt * lc,), i32),
                   jax.ShapeDtypeStruct((nb,), i32), jax.ShapeDtypeStruct((1,), i32)),
        in_specs=[smem],
        out_specs=(smem, smem, smem, smem),
        scratch_shapes=[pltpu.SMEM((nt,), i32)],
        name="moe_index",
    )(ch)
    ys = _expert_mlp(xs, src.reshape(nb, 1, MOE_CHUNKS), block_expert, n_used,
                     p["w_gate"], p["w_up"], p["w_down"], layer)
    return _combine(ys, back.reshape(nt, 1, lc), route, x1, p["ln2_g"], p["ln2_b"], tm)


def _layer_params(l, w_in, b_in, conv_w, sgu_ln_g, sgu_ln_b, sgu_w, sgu_b, w_branch, w_out, ln1_g, ln1_b,
                  w_group, b_group, w_router, b_router, w_gate, w_up, w_down, ln2_g, ln2_b, tm):
    offs = [0]
    for s in IN_SIZES:
        offs.append(offs[-1] + s)
    wt, b = w_in[l].T, b_in[l]
    seg = lambda a, i: a[offs[i]:offs[i + 1]]
    fox_wt, fox_b = seg(wt, 0), seg(b, 0)
    fq_wt, fk_wt, fv_wt = (fox_wt[j * FOX_WIDTH:(j + 1) * FOX_WIDTH] for j in range(3))
    fq_b, fk_b, fv_b = (fox_b[j * FOX_WIDTH:(j + 1) * FOX_WIDTH] for j in range(3))
    wmt = jnp.concatenate([fk_wt, seg(wt, 2), seg(wt, 3), seg(wt, 4), seg(wt, 6), seg(wt, 7)], axis=0)
    bm = jnp.concatenate([fk_b, seg(b, 2), seg(b, 3), seg(b, 4), seg(b, 6), seg(b, 7)])
    n_small = IN_SIZES[1] + IN_SIZES[5]
    wst = jnp.concatenate([seg(wt, 1), seg(wt, 5)], axis=0).astype(BF16)
    ws = jnp.concatenate([wst, jnp.zeros((LANES - n_small, D_MODEL), BF16)], axis=0).T
    bs = jnp.concatenate([seg(b, 1), seg(b, 5), jnp.zeros((LANES - n_small,), F32)])
    reps = SGU_BLOCK // SGU_CHUNK
    sgu_bias = jnp.broadcast_to(sgu_b[l].T[:, :, None], (SGU_CHUNK, SGU_GROUPS, SGU_WIDTH // SGU_GROUPS))
    sgu_bias = jnp.tile(sgu_bias.reshape(SGU_CHUNK, SGU_WIDTH), (tm // SGU_CHUNK, 1))
    n_route = N_GROUPS + N_EXPERTS
    wr = jnp.concatenate([w_group[l], w_router[l], jnp.zeros((D_MODEL, LANES - n_route), F32)], axis=1)
    br = jnp.concatenate([b_group[l], b_router[l], jnp.zeros((LANES - n_route,), F32)])
    return {
        "wm": wmt.astype(BF16).T, "bm": bm.reshape(1, -1),
        "wqvt": jnp.concatenate([fq_wt, fv_wt], axis=0).astype(BF16),
        "bqvt": jnp.concatenate([fq_b, fv_b]).reshape(-1, 1),
        "ws": ws, "bs": bs.reshape(1, -1),
        "wst": wst, "bst": bs[:SM_ROWS].reshape(-1, 1),
        "conv_w": conv_w[l],
        "sgu_g": sgu_ln_g[l].reshape(1, -1), "sgu_b": sgu_ln_b[l].reshape(1, -1),
        "sgu_wt": jnp.tile(sgu_w[l], (1, reps, reps)), "sgu_bias": sgu_bias,
        "wb": w_branch[l].astype(BF16), "wo": w_out[l].astype(BF16),
        "ln1_g": ln1_g[l].reshape(1, -1), "ln1_b": ln1_b[l].reshape(1, -1),
        "wr": wr.astype(BF16), "br": br.reshape(1, -1),
        "w_gate": w_gate, "w_up": w_up, "w_down": w_down,
        "ln2_g": ln2_g[l].reshape(1, -1), "ln2_b": ln2_b[l].reshape(1, -1),
    }


def _tiles(B, S):
    T = B * S
    tm = 512 if T % 512 == 0 else 256
    tq = 512 if S % 512 == 0 else 256
    return tm, tq


def kernel(x, ln_in_g, ln_in_b, w_in, b_in, conv_w, sgu_ln_g, sgu_ln_b, sgu_w, sgu_b, w_branch, w_out,
           ln1_g, ln1_b, w_group, b_group, w_router, b_router, w_gate, w_up, w_down, ln2_g, ln2_b):
    B, S, D = x.shape
    assert D == D_MODEL and S % SGU_BLOCK == 0
    T = B * S
    tm, tq = _tiles(B, S)
    h = _input_layer_norm(x.reshape(T, D), ln_in_g, ln_in_b, tm)
    for l in range(DEPTH):
        p = _layer_params(l, w_in, b_in, conv_w, sgu_ln_g, sgu_ln_b, sgu_w, sgu_b, w_branch, w_out, ln1_g,
                          ln1_b, w_group, b_group, w_router, b_router, w_gate, w_up, w_down, ln2_g, ln2_b, tm)
        qt, fk, vt, mqk, mv, mo, yc, gates, sm, smt = _in_projection(h, p, tm)
        fcol, frow = _forget_cumsum(sm, smt, B, S)
        ya = _fox_attention(qt, fk, vt, fcol, frow, B, S, tq)
        yb = _mlstm(mqk, mv, mo, sm, smt, p["conv_w"], B, S)
        x1, xs, route, counts = _merge_project_route(ya, yb, yc, gates, h, p, tm)
        h = _moe(x1, xs, route, counts, p, l, tm)
    return h.reshape(B, S, D)
```

```python
import functools

import jax
import jax.numpy as jnp
from jax import lax
from jax.experimental import pallas as pl
from jax.experimental.pallas import tpu as pltpu

F32 = jnp.float32
BF16 = jnp.bfloat16

D_MODEL = 1024
DEPTH = 2
FOX_HEADS = 8
FOX_HEAD_DIM = 64
FOX_WIDTH = FOX_HEADS * FOX_HEAD_DIM
MLSTM_HEADS = 4
MLSTM_HEAD_DIM = 128
MLSTM_WIDTH = MLSTM_HEADS * MLSTM_HEAD_DIM
MLSTM_CHUNK = 128
MLSTM_CONV = 4
SGU_GROUPS = 4
SGU_WIDTH = 512
SGU_CHUNK = 128
N_BRANCH = 3
N_GROUPS = 4
EXPERTS_PER_GROUP = 8
N_EXPERTS = N_GROUPS * EXPERTS_PER_GROUP
TOP_K = 2
D_EXPERT = 512
DEEPNORM_ALPHA = (2 * DEPTH) ** 0.25
LN_EPS = 1e-5
NEG_INF = -1e30

IN_SIZES = (3 * FOX_WIDTH, FOX_HEADS, 2 * MLSTM_WIDTH, MLSTM_WIDTH, MLSTM_WIDTH,
            2 * MLSTM_HEADS, 2 * SGU_WIDTH, N_BRANCH * D_MODEL)

LANES = 128
SUBLANES = 8
VMEM_LIMIT_BYTES = 56 * 1024 * 1024

SM_FOXF = 0
SM_MI = 8
SM_MF = 12
SM_ROWS = 16

RT_E1, RT_E2, RT_P1, RT_P2 = 0, 1, 2, 3
RT_WIDTH = 8
ROUTER_EXPERT_LANE0 = N_GROUPS

MOE_ROWS = 256
MOE_ALIGN = 16
MOE_CHUNKS = MOE_ROWS // MOE_ALIGN
EXPERT_BUFFERS = 3
XS_G1, XS_G2, XS_E1 = 0, 3, 6
XS_WIDTH = D_MODEL + LANES
SGU_BLOCK = 256
CUMSUM_BLOCK = 256


def _params(sem, vmem=VMEM_LIMIT_BYTES):
    return pltpu.CompilerParams(dimension_semantics=sem, vmem_limit_bytes=vmem)


def _resident(shape, index_map):
    return pl.BlockSpec(shape, index_map, pipeline_mode=pl.Buffered(1))


def _layer_norm(x, g, b):
    mu = jnp.mean(x, axis=-1, keepdims=True)
    xc = x - mu
    var = jnp.mean(xc * xc, axis=-1, keepdims=True)
    return xc * lax.rsqrt(var + LN_EPS) * g + b


def _log_sigmoid(x):
    return jnp.minimum(x, 0.0) - jnp.log1p(jnp.exp(-jnp.abs(x)))


def _gelu_tanh(x):
    return 0.5 * x * (1.0 + jnp.tanh(0.7978845608028654 * (x + 0.044715 * x * x * x)))


def _dot(a, b):
    return jnp.dot(a, b, preferred_element_type=F32)


def _dot_nt(a, b):
    return lax.dot_general(a, b, (((1,), (1,)), ((), ())), preferred_element_type=F32)


def _dot_tn(a, b):
    return lax.dot_general(a, b, (((0,), (0,)), ((), ())), preferred_element_type=F32)


def _dot_f32(a, b):
    return jnp.dot(a, b, preferred_element_type=F32, precision=lax.Precision.HIGHEST)


def _ln_kernel(x_ref, g_ref, b_ref, o_ref):
    o_ref[...] = _layer_norm(x_ref[...], g_ref[...], b_ref[...])


def _input_layer_norm(x, g, b, tm):
    T, D = x.shape
    return pl.pallas_call(
        _ln_kernel,
        out_shape=jax.ShapeDtypeStruct((T, D), F32),
        grid=(T // tm,),
        in_specs=[pl.BlockSpec((tm, D), lambda i: (i, 0)),
                  pl.BlockSpec((1, D), lambda i: (0, 0)),
                  pl.BlockSpec((1, D), lambda i: (0, 0))],
        out_specs=pl.BlockSpec((tm, D), lambda i: (i, 0)),
        compiler_params=_params(("parallel",)),
        name="ln_in",
    )(x, g.reshape(1, D), b.reshape(1, D))


C_FK = 0
C_MQK = C_FK + FOX_WIDTH
C_MV = C_MQK + 2 * MLSTM_WIDTH
C_MO = C_MV + MLSTM_WIDTH
C_SU = C_MO + MLSTM_WIDTH
C_SV = C_SU + SGU_WIDTH
C_GATE = C_SV + SGU_WIDTH
C_END = C_GATE + N_BRANCH * D_MODEL
PROJ_PIECE = 512


def _inproj_kernel(x_ref, wm_ref, bm_ref, wqvt_ref, bqvt_ref, ws_ref, bs_ref, wst_ref, bst_ref,
                   sg_ref, sb_ref, swt_ref, sbias_ref,
                   qt_ref, fk_ref, vt_ref, mqk_ref, mv_ref, mo_ref, yc_ref, gt_ref, sm_ref, smt_ref,
                   wbd_scr):
    tm = x_ref.shape[0]

    @pl.when(pl.program_id(0) == 0)
    def _():
        r = lax.broadcasted_iota(jnp.int32, (SGU_BLOCK, SGU_BLOCK), 0)
        c = lax.broadcasted_iota(jnp.int32, (SGU_BLOCK, SGU_BLOCK), 1)
        keep = jnp.logical_and(r // SGU_CHUNK == c // SGU_CHUNK, r >= c)
        for g in range(SGU_GROUPS):
            wbd_scr[g] = jnp.where(keep, swt_ref[g], 0.0).astype(BF16)

    xb = x_ref[...].astype(BF16)

    def proj(lo, width=PROJ_PIECE):
        return _dot(xb, wm_ref[:, lo:lo + width]) + bm_ref[:, lo:lo + width]

    fk_ref[...] = proj(C_FK).astype(BF16)
    qvt = _dot_nt(wqvt_ref[...], xb) + bqvt_ref[...]
    qt_ref[...] = (qvt[:FOX_WIDTH] * (LOG2E * FOX_HEAD_DIM ** -0.5)).astype(BF16)
    vt_ref[...] = qvt[FOX_WIDTH:].astype(BF16)
    for j in range(2):
        mqk_ref[:, j * PROJ_PIECE:(j + 1) * PROJ_PIECE] = proj(C_MQK + j * PROJ_PIECE).astype(BF16)
    mv_ref[...] = proj(C_MV).astype(BF16)
    mo_ref[...] = jax.nn.sigmoid(proj(C_MO)).astype(BF16)
    for j in range(N_BRANCH * D_MODEL // PROJ_PIECE):
        gt_ref[:, j * PROJ_PIECE:(j + 1) * PROJ_PIECE] = jax.nn.sigmoid(
            proj(C_GATE + j * PROJ_PIECE)).astype(BF16)

    sm = _dot(xb, ws_ref[...]) + bs_ref[...]
    lane = lax.broadcasted_iota(jnp.int32, sm.shape, 1)
    is_ls = jnp.logical_or(lane < SM_MI, jnp.logical_and(lane >= SM_MF, lane < SM_ROWS))
    sm_ref[...] = jnp.where(is_ls, _log_sigmoid(sm), sm)
    smt = _dot_nt(wst_ref[...], xb) + bst_ref[...]
    row = lax.broadcasted_iota(jnp.int32, smt.shape, 0)
    is_ls_t = jnp.logical_or(row < SM_MI, row >= SM_MF)
    smt_ref[...] = jnp.where(is_ls_t, _log_sigmoid(smt), smt)

    u = _gelu_tanh(proj(C_SU))
    v = _gelu_tanh(proj(C_SV))
    vn = _layer_norm(v, sg_ref[...], sb_ref[...]).astype(BF16)
    for rb in range(tm // SGU_BLOCK):
        rows = slice(rb * SGU_BLOCK, (rb + 1) * SGU_BLOCK)
        for g in range(SGU_GROUPS):
            cols = slice(g * LANES, (g + 1) * LANES)
            z = _dot(wbd_scr[g], vn[rows, cols]) + sbias_ref[rows, cols]
            yc_ref[rows, cols] = (u[rows, cols] * z).astype(BF16)


def _in_projection(x, p, tm):
    T, D = x.shape
    row = lambda w: pl.BlockSpec((tm, w), lambda i: (i, 0))
    out_shapes = (
        jax.ShapeDtypeStruct((FOX_WIDTH, T), BF16),
        jax.ShapeDtypeStruct((T, FOX_WIDTH), BF16),
        jax.ShapeDtypeStruct((FOX_WIDTH, T), BF16),
        jax.ShapeDtypeStruct((T, 2 * MLSTM_WIDTH), BF16),
        jax.ShapeDtypeStruct((T, MLSTM_WIDTH), BF16),
        jax.ShapeDtypeStruct((T, MLSTM_WIDTH), BF16),
        jax.ShapeDtypeStruct((T, SGU_WIDTH), BF16),
        jax.ShapeDtypeStruct((T, N_BRANCH * D_MODEL), BF16),
        jax.ShapeDtypeStruct((T, LANES), F32),
        jax.ShapeDtypeStruct((SM_ROWS, T), F32),
    )
    col = lambda h: pl.BlockSpec((h, tm), lambda i: (0, i))
    out_specs = (col(FOX_WIDTH), row(FOX_WIDTH), col(FOX_WIDTH),
                 row(2 * MLSTM_WIDTH), row(MLSTM_WIDTH), row(MLSTM_WIDTH), row(SGU_WIDTH),
                 row(N_BRANCH * D_MODEL), row(LANES), col(SM_ROWS))
    const = lambda shape: _resident(shape, lambda i: tuple(0 for _ in shape))
    in_specs = [
        row(D),
        const((D, C_END)), const((1, C_END)),
        const((2 * FOX_WIDTH, D)), const((2 * FOX_WIDTH, 1)),
        const((D, LANES)), const((1, LANES)),
        const((SM_ROWS, D)), const((SM_ROWS, 1)),
        const((1, SGU_WIDTH)), const((1, SGU_WIDTH)),
        const((SGU_GROUPS, SGU_BLOCK, SGU_BLOCK)),
        const((tm, SGU_WIDTH)),
    ]
    return pl.pallas_call(
        _inproj_kernel,
        out_shape=out_shapes,
        grid=(T // tm,),
        in_specs=in_specs,
        out_specs=out_specs,
        scratch_shapes=[pltpu.VMEM((SGU_GROUPS, SGU_BLOCK, SGU_BLOCK), BF16)],
        compiler_params=_params(("arbitrary",)),
        name="in_proj",
    )(x, p["wm"], p["bm"], p["wqvt"], p["bqvt"], p["ws"], p["bs"], p["wst"], p["bst"],
      p["sgu_g"], p["sgu_b"], p["sgu_wt"], p["sgu_bias"])


def _fcum_kernel(sm_ref, smt_ref, fcol_ref, frow_ref):
    S = sm_ref.shape[0]
    cb = min(CUMSUM_BLOCK, S)
    r = lax.broadcasted_iota(jnp.int32, (cb, cb), 0)
    c = lax.broadcasted_iota(jnp.int32, (cb, cb), 1)
    lower = (r >= c).astype(F32)
    upper = (r <= c).astype(F32)
    carry_c = jnp.zeros((1, LANES), F32)
    carry_r = jnp.zeros((SM_ROWS, 1), F32)
    for blk in range(S // cb):
        sl = slice(blk * cb, (blk + 1) * cb)
        fc = _dot_f32(lower, sm_ref[sl, :]) + carry_c
        fcol_ref[sl, :] = fc
        carry_c = fc[cb - 1:cb, :]
        fr = _dot_f32(smt_ref[:, sl], upper) + carry_r
        frow_ref[:, sl] = fr
        carry_r = fr[:, cb - 1:cb]


def _forget_cumsum(sm, smt, B, S):
    T = B * S
    return pl.pallas_call(
        _fcum_kernel,
        out_shape=(jax.ShapeDtypeStruct((T, LANES), F32), jax.ShapeDtypeStruct((SM_ROWS, T), F32)),
        grid=(B,),
        in_specs=[pl.BlockSpec((S, LANES), lambda b: (b, 0)),
                  pl.BlockSpec((SM_ROWS, S), lambda b: (0, b))],
        out_specs=(pl.BlockSpec((S, LANES), lambda b: (b, 0)),
                   pl.BlockSpec((SM_ROWS, S), lambda b: (0, b))),
        compiler_params=_params(("parallel",)),
        name="forget_cumsum",
    )(sm, smt)


N_SPLIT = 3
FOX_ONES_ROWS = 16
LOG2E = 1.4426950408889634


def _split3(f):
    hi = f.astype(BF16).astype(F32)
    r1 = f - hi
    mid = r1.astype(BF16).astype(F32)
    lo = r1 - mid
    return hi, mid, lo


def _fox_kernel(qt_ref, k_ref, vt_ref, fcol_ref, frow_ref, o_ref, kaug_scr, vone_scr, s_scr, m_scr, acc_scr, *, tq):
    S = k_ref.shape[0]
    nq = S // tq
    pair = pl.program_id(1)

    lane = lax.broadcasted_iota(jnp.int32, (S, LANES), 1)
    fc = fcol_ref[...]
    k = k_ref[...]
    for hh in range(2):
        f = jnp.sum(jnp.where(lane == 2 * pair + hh, fc, 0.0), axis=1, keepdims=True)
        hi, mid, lo = _split3(LOG2E * f)
        aug = jnp.where(lane < N_SPLIT, 1.0,
                        jnp.where(lane == N_SPLIT, -hi,
                                  jnp.where(lane == N_SPLIT + 1, -mid,
                                            jnp.where(lane == N_SPLIT + 2, -lo, 0.0))))
        kaug_scr[hh, :, 0:LANES] = k
        kaug_scr[hh, :, LANES:2 * LANES] = aug.astype(BF16)
    vone_scr[0:LANES, :] = vt_ref[...]
    vone_scr[LANES:, :] = jnp.ones((FOX_ONES_ROWS, S), BF16)

    sub = lax.broadcasted_iota(jnp.int32, (LANES, tq), 0)
    key = lax.broadcasted_iota(jnp.int32, (tq, tq), 0)
    qry = lax.broadcasted_iota(jnp.int32, (tq, tq), 1)
    causal = key <= qry

    def start_tile(qi):
        cols = slice(qi * tq, (qi + 1) * tq)
        qt = qt_ref[:, cols]
        rhs = []
        for hh in range(2):
            hi, mid, lo = _split3(LOG2E * frow_ref[hh][:, cols])
            augq = jnp.where(sub == 0, hi,
                             jnp.where(sub == 1, mid,
                                       jnp.where(sub == 2, lo,
                                                 jnp.where(sub < 2 * N_SPLIT, 1.0, 0.0))))
            in_head = (sub < FOX_HEAD_DIM) if hh == 0 else (sub >= FOX_HEAD_DIM)
            qm = jnp.where(in_head, qt, jnp.zeros_like(qt))
            rhs.append(jnp.concatenate([qm, augq.astype(BF16)], axis=0))
            m_scr[qi % 2, hh] = jnp.full((1, tq), NEG_INF, F32)
            acc_scr[qi % 2, hh] = jnp.zeros((LANES + FOX_ONES_ROWS, tq), F32)
        return rhs

    def scores(rhs, j, hh, slot):
        s_scr[slot, hh] = _dot(kaug_scr[hh, j * tq:(j + 1) * tq, :], rhs[hh])

    def consume(qi, j, hh, slot):
        par = qi % 2
        s = s_scr[slot, hh]
        if j == qi:
            s = jnp.where(causal, s, NEG_INF)
        m = m_scr[par, hh]
        m_new = jnp.maximum(m, jnp.max(s, axis=0, keepdims=True))
        p = jnp.exp2((s - m_new).astype(BF16))
        corr = jnp.exp2(m - m_new)
        m_scr[par, hh] = m_new
        acc_scr[par, hh] = acc_scr[par, hh] * corr + _dot(vone_scr[:, j * tq:(j + 1) * tq], p)

    steps = [(qi, j) for qi in range(nq) for j in range(qi + 1)]
    rhs = start_tile(0)
    for hh in range(2):
        scores(rhs, 0, hh, 0)
    for t, (qi, j) in enumerate(steps):
        slot = t % 2
        nxt = steps[t + 1] if t + 1 < len(steps) else None
        rhs_next = rhs
        if nxt is not None and nxt[1] == 0:
            rhs_next = start_tile(nxt[0])
        for hh in range(2):
            if nxt is not None:
                scores(rhs_next, nxt[1], hh, 1 - slot)
            consume(qi, j, hh, slot)
        if j == qi:
            outs = [acc_scr[qi % 2, hh, 0:LANES, :] / acc_scr[qi % 2, hh, LANES:LANES + 1, :] for hh in range(2)]
            out_t = jnp.where(sub < FOX_HEAD_DIM, outs[0], outs[1])
            o_ref[qi * tq:(qi + 1) * tq, :] = out_t.T.astype(o_ref.dtype)
        rhs = rhs_next


def _fox_attention(qt, fk, vt, fcol, frow, B, S, tq):
    T = B * S
    n_pairs = FOX_HEADS // 2
    frow3 = frow.reshape(SM_ROWS, 1, T)
    return pl.pallas_call(
        functools.partial(_fox_kernel, tq=tq),
        out_shape=jax.ShapeDtypeStruct((T, FOX_WIDTH), BF16),
        grid=(B, n_pairs),
        in_specs=[pl.BlockSpec((LANES, S), lambda b, p: (p, b)),
                  pl.BlockSpec((S, LANES), lambda b, p: (b, p)),
                  pl.BlockSpec((LANES, S), lambda b, p: (p, b)),
                  pl.BlockSpec((S, LANES), lambda b, p: (b, 0)),
                  pl.BlockSpec((2, 1, S), lambda b, p: (p, 0, b))],
        out_specs=pl.BlockSpec((S, LANES), lambda b, p: (b, p)),
        scratch_shapes=[pltpu.VMEM((2, S, 2 * LANES), BF16),
                        pltpu.VMEM((LANES + FOX_ONES_ROWS, S), BF16),
                        pltpu.VMEM((2, 2, tq, tq), F32),
                        pltpu.VMEM((2, 2, 1, tq), F32),
                        pltpu.VMEM((2, 2, LANES + FOX_ONES_ROWS, tq), F32)],
        compiler_params=_params(("parallel", "parallel")),
        name="fox_attention",
    )(qt, fk, vt, fcol, frow3)


def _mlstm_kernel(qk_ref, v_ref, og_ref, sm_ref, smt_ref, cw_ref, y_ref, c_scr, n_scr, m_scr, tail_scr):
    L = MLSTM_CHUNK
    dh = MLSTM_HEAD_DIM

    @pl.when(pl.program_id(1) == 0)
    def _():
        c_scr[...] = jnp.zeros_like(c_scr)
        n_scr[...] = jnp.zeros_like(n_scr)
        m_scr[...] = jnp.zeros_like(m_scr)
        tail_scr[...] = jnp.zeros_like(tail_scr)

    x = qk_ref[...].astype(F32)
    xe = jnp.concatenate([tail_scr[...], x], axis=0)
    cw = cw_ref[...]
    y = x * cw[MLSTM_CONV - 1:MLSTM_CONV, :]
    for k in range(1, MLSTM_CONV):
        shifted = pltpu.roll(xe, k, 0)[SUBLANES:, :]
        y = y + shifted * cw[MLSTM_CONV - 1 - k:MLSTM_CONV - k, :]
    tail_scr[...] = x[L - SUBLANES:, :]
    y = y * jax.nn.sigmoid(y)
    q_all = y[:, :MLSTM_WIDTH]
    k_all = y[:, MLSTM_WIDTH:] * (dh ** -0.5)

    sm = sm_ref[...]
    smt = smt_ref[...]
    r = lax.broadcasted_iota(jnp.int32, (L, L), 0)
    c = lax.broadcasted_iota(jnp.int32, (L, L), 1)
    causal = r >= c
    bcol_all = _dot_f32(causal.astype(F32), sm)
    brow_all = _dot_f32(smt, (r <= c).astype(F32))

    heads = range(MLSTM_HEADS)
    hsl = [slice(h * dh, (h + 1) * dh) for h in heads]
    qh = [q_all[:, hsl[h]] for h in heads]
    qb = [qh[h].astype(BF16) for h in heads]
    kh = [k_all[:, hsl[h]] for h in heads]
    vh = [v_ref[:, hsl[h]] for h in heads]
    c_prev = [c_scr[h] for h in heads]
    n_prev = [n_scr[h:h + 1, :] for h in heads]
    m_prev = [m_scr[h:h + 1, 0:1] for h in heads]
    s_qk = [_dot_nt(qb[h], kh[h].astype(BF16)) for h in heads]
    q_c = [_dot(qb[h], c_prev[h].astype(BF16)) for h in heads]

    bq = [bcol_all[:, SM_MF + h:SM_MF + h + 1] for h in heads]
    li_c = [sm[:, SM_MI + h:SM_MI + h + 1] for h in heads]
    br = [brow_all[SM_MF + h:SM_MF + h + 1, :] for h in heads]
    li_r = [smt[SM_MI + h:SM_MI + h + 1, :] for h in heads]
    b_last = [bq[h][L - 1:L, :] for h in heads]
    d = [jnp.where(causal, bq[h] - br[h] + li_r[h], NEG_INF) for h in heads]
    inter = [bq[h] + m_prev[h] for h in heads]
    m_t = [jnp.maximum(inter[h], jnp.max(d[h], axis=1, keepdims=True)) for h in heads]
    w_intra = [jnp.exp(d[h] - m_t[h]) for h in heads]
    w_inter = [jnp.exp(inter[h] - m_t[h]) for h in heads]

    qk = [s_qk[h] * w_intra[h] for h in heads]
    num = [_dot(qk[h].astype(BF16), vh[h]) + w_inter[h] * q_c[h] for h in heads]
    den = [jnp.sum(qk[h], axis=1, keepdims=True)
           + w_inter[h] * jnp.sum(qh[h] * n_prev[h], axis=1, keepdims=True) for h in heads]
    for h in heads:
        h_c = num[h] / jnp.maximum(jnp.abs(den[h]), jnp.exp(-m_t[h]))
        y_ref[:, hsl[h]] = (og_ref[:, hsl[h]].astype(F32) * h_c).astype(y_ref.dtype)

    g_c = [b_last[h] - bq[h] + li_c[h] for h in heads]
    m_new = [jnp.maximum(b_last[h] + m_prev[h], jnp.max(g_c[h], axis=0, keepdims=True)) for h in heads]
    decay = [jnp.exp(b_last[h] + m_prev[h] - m_new[h]) for h in heads]
    kw = [kh[h] * jnp.exp(g_c[h] - m_new[h]) for h in heads]
    for h in heads:
        c_scr[h] = decay[h] * c_prev[h] + _dot_tn(kw[h].astype(BF16), vh[h])
        n_scr[h:h + 1, :] = decay[h] * n_prev[h] + jnp.sum(kw[h], axis=0, keepdims=True)
        m_scr[h:h + 1, :] = jnp.broadcast_to(m_new[h], (1, LANES))


def _mlstm(mqk, mv, mo, sm, smt, conv_w, B, S):
    T = B * S
    L = MLSTM_CHUNK
    nc = S // L
    row = lambda w: pl.BlockSpec((L, w), lambda b, c: (b * nc + c, 0))
    return pl.pallas_call(
        _mlstm_kernel,
        out_shape=jax.ShapeDtypeStruct((T, MLSTM_WIDTH), BF16),
        grid=(B, nc),
        in_specs=[row(2 * MLSTM_WIDTH), row(MLSTM_WIDTH), row(MLSTM_WIDTH), row(LANES),
                  pl.BlockSpec((SM_ROWS, L), lambda b, c: (0, b * nc + c)),
                  pl.BlockSpec((MLSTM_CONV, 2 * MLSTM_WIDTH), lambda b, c: (0, 0))],
        out_specs=row(MLSTM_WIDTH),
        scratch_shapes=[pltpu.VMEM((MLSTM_HEADS, MLSTM_HEAD_DIM, MLSTM_HEAD_DIM), F32),
                        pltpu.VMEM((SUBLANES, MLSTM_HEAD_DIM), F32),
                        pltpu.VMEM((SUBLANES, LANES), F32),
                        pltpu.VMEM((SUBLANES, 2 * MLSTM_WIDTH), F32)],
        compiler_params=_params(("parallel", "arbitrary")),
        name="mlstm",
    )(mqk, mv, mo, sm, smt, conv_w)


def _merge_kernel(ya_ref, yb_ref, yc_ref, gt_ref, x_ref, wb_ref, wo_ref, g_ref, b_ref, wr_ref, br_ref,
                  x1_ref, xs_ref, rt_ref, cnt_ref, x1_scr):
    tm = x_ref.shape[0]

    @pl.when(pl.program_id(0) == 0)
    def _():
        x1_scr[...] = jnp.zeros_like(x1_scr)

    xp = x1_scr[...]
    xpb = xp.astype(BF16)
    lane = lax.broadcasted_iota(jnp.int32, (tm, LANES), 1)
    big = jnp.int32(LANES)

    def branch(n, y_ref):
        return _dot(y_ref[...], wb_ref[n]) * gt_ref[:, n * D_MODEL:(n + 1) * D_MODEL].astype(F32)

    merged = branch(0, ya_ref)

    logits = _dot(xpb, wr_ref[...]) + br_ref[...]
    glog = jnp.where(lane < N_GROUPS, logits, -jnp.inf)
    gmax = jnp.max(glog, axis=1, keepdims=True)
    g_top = jnp.min(jnp.where(glog == gmax, lane, big), axis=1, keepdims=True)
    p_g = 1.0 / jnp.sum(jnp.exp(glog - gmax), axis=1, keepdims=True)

    merged = merged + branch(1, yb_ref)

    lo = ROUTER_EXPERT_LANE0 + EXPERTS_PER_GROUP * g_top
    el = jnp.where(jnp.logical_and(lane >= lo, lane < lo + EXPERTS_PER_GROUP), logits, -jnp.inf)
    m1 = jnp.max(el, axis=1, keepdims=True)
    i1 = jnp.min(jnp.where(el == m1, lane, big), axis=1, keepdims=True)
    el2 = jnp.where(lane == i1, -jnp.inf, el)
    m2 = jnp.max(el2, axis=1, keepdims=True)
    i2 = jnp.min(jnp.where(el2 == m2, lane, big), axis=1, keepdims=True)
    ratio = jnp.exp(m2 - m1)
    gate1 = p_g / (1.0 + ratio)
    gate2 = p_g * ratio / (1.0 + ratio)

    merged = merged + branch(2, yc_ref)

    hit1 = lane == i1
    hit2 = lane == i2
    onehot = jnp.where(jnp.logical_or(hit1, hit2), 1.0, 0.0)
    r = lax.broadcasted_iota(jnp.int32, (tm, tm), 0)
    c = lax.broadcasted_iota(jnp.int32, (tm, tm), 1)
    before = jnp.where(r > c, 1.0, 0.0).astype(BF16)
    seen = _dot(before, onehot.astype(BF16))
    count = jnp.sum(onehot, axis=0, keepdims=True)
    chunks = jnp.floor((count + (MOE_ALIGN - 1)) * (1.0 / MOE_ALIGN))
    er = lax.broadcasted_iota(jnp.int32, (LANES, LANES), 0)
    ec = lax.broadcasted_iota(jnp.int32, (LANES, LANES), 1)
    earlier = jnp.where(er < ec, 1.0, 0.0)
    start = MOE_ALIGN * _dot_f32(jnp.broadcast_to(chunks, (SUBLANES, LANES)), earlier)[0:1, :]
    where_to = start + seen
    pos1 = jnp.sum(jnp.where(hit1, where_to, 0.0), axis=1, keepdims=True)
    pos2 = jnp.sum(jnp.where(hit2, where_to, 0.0), axis=1, keepdims=True)
    cnt_ref[0] = jnp.broadcast_to(count, (SUBLANES, LANES))

    e1 = (i1 - ROUTER_EXPERT_LANE0).astype(F32)
    e2 = (i2 - ROUTER_EXPERT_LANE0).astype(F32)
    rec = jnp.zeros((tm, LANES), F32)
    for pos, val in ((RT_E1, e1), (RT_E2, e2), (RT_P1, pos1), (RT_P2, pos2)):
        rec = jnp.where(lane == pos, val, rec)
    rt_ref[...] = rec[:, :RT_WIDTH]

    mix = _dot(merged.astype(BF16), wo_ref[...])

    tag = jnp.zeros((tm, LANES), F32)
    pieces = _split3(gate1) + _split3(gate2) + (e1,)
    for pos, val in enumerate(pieces):
        tag = jnp.where(lane == pos, val, tag)
    xa = jnp.concatenate([xpb, tag.astype(BF16)], axis=1)
    n_local = xs_ref.shape[0]
    col = lax.broadcasted_iota(jnp.int32, (tm, n_local), 1)
    sel = jnp.logical_or(col == pos1.astype(jnp.int32), col == pos2.astype(jnp.int32))
    xs_ref[...] = _dot_tn(jnp.where(sel, 1.0, 0.0).astype(BF16), xa).astype(BF16)

    x1 = _layer_norm(DEEPNORM_ALPHA * x_ref[...] + mix, g_ref[...], b_ref[...])
    x1_ref[...] = x1
    x1_scr[...] = x1


def _local_rows(tm):
    worst = TOP_K * tm + N_EXPERTS * (MOE_ALIGN - 1)
    return -(-worst // LANES) * LANES


def _merge_project_route(ya, yb, yc, gates, x, p, tm):
    T, D = x.shape
    nt = T // tm
    n_local = _local_rows(tm)
    cur = lambda i: jnp.minimum(i, nt - 1)
    prev = lambda i: jnp.maximum(i - 1, 0)
    row = lambda w: pl.BlockSpec((tm, w), lambda i: (cur(i), 0))
    const = lambda shape: _resident(shape, lambda i: tuple(0 for _ in shape))
    return pl.pallas_call(
        _merge_kernel,
        out_shape=(jax.ShapeDtypeStruct((T, D), F32),
                   jax.ShapeDtypeStruct((nt * n_local, XS_WIDTH), BF16),
                   jax.ShapeDtypeStruct((T, RT_WIDTH), F32),
                   jax.ShapeDtypeStruct((nt, SUBLANES, LANES), F32)),
        grid=(nt + 1,),
        in_specs=[row(FOX_WIDTH), row(MLSTM_WIDTH), row(SGU_WIDTH), row(N_BRANCH * D), row(D),
                  const((N_BRANCH, FOX_WIDTH, D)), const((D, D)), const((1, D)), const((1, D)),
                  const((D, LANES)), const((1, LANES))],
        out_specs=(row(D), pl.BlockSpec((n_local, XS_WIDTH), lambda i: (prev(i), 0)),
                   pl.BlockSpec((tm, RT_WIDTH), lambda i: (prev(i), 0)),
                   pl.BlockSpec((1, SUBLANES, LANES), lambda i: (prev(i), 0, 0))),
        scratch_shapes=[pltpu.VMEM((tm, D), F32)],
        compiler_params=_params(("arbitrary",)),
        name="merge_route",
    )(ya, yb, yc, gates, x, p["wb"], p["wo"], p["ln1_g"], p["ln1_b"], p["wr"], p["br"])


def _chunk_gather(src_hbm, idx_ref, dst, sem, n_chunks):
    for j in range(n_chunks):
        start = pl.multiple_of(idx_ref[0, 0, j] * MOE_ALIGN, MOE_ALIGN)
        pltpu.make_async_copy(src_hbm.at[pl.ds(start, MOE_ALIGN)], dst.at[pl.ds(j * MOE_ALIGN, MOE_ALIGN)],
                              sem).start(priority=j % 2)


def _chunk_gather_wait(src_hbm, dst, sem):
    n = dst.shape[0]
    pltpu.make_async_copy(src_hbm.at[pl.ds(0, n)], dst, sem).wait()


def _expert_kernel(be_ref, nu_ref, src0_ref, src1_ref, src2_ref, xs_hbm, wg_ref, wu_ref, wd_ref, y_ref,
                   xbuf, wg_scr, wu_scr, wd_scr, gsem):
    i = pl.program_id(0)
    n_used = nu_ref[0]
    slot = lax.rem(i, EXPERT_BUFFERS)
    active = i < n_used

    for ahead, idx_ref in ((0, src0_ref), (1, src1_ref)):
        @pl.when(jnp.logical_and(i == 0, ahead < n_used))
        def _(ahead=ahead, idx_ref=idx_ref):
            _chunk_gather(xs_hbm, idx_ref, xbuf.at[ahead], gsem.at[ahead], MOE_CHUNKS)

    @pl.when(i + 2 < n_used)
    def _():
        nxt = lax.rem(i + 2, EXPERT_BUFFERS)
        _chunk_gather(xs_hbm, src2_ref, xbuf.at[nxt], gsem.at[nxt], MOE_CHUNKS)

    new_expert = jnp.logical_or(i == 0, be_ref[i] != be_ref[jnp.maximum(i - 1, 0)])

    @pl.when(jnp.logical_and(active, new_expert))
    def _():
        wg_scr[...] = wg_ref[...].astype(BF16)
        wu_scr[...] = wu_ref[...].astype(BF16)
        wd_scr[...] = wd_ref[...].astype(BF16)

    @pl.when(active)
    def _():
        _chunk_gather_wait(xs_hbm, xbuf.at[slot], gsem.at[slot])
        xa = xbuf[slot]
        xb = xa[:, :D_MODEL]
        tag = xa[:, D_MODEL:].astype(F32)
        lane = lax.broadcasted_iota(jnp.int32, tag.shape, 1)
        pick = lambda lo, hi: jnp.sum(jnp.where(jnp.logical_and(lane >= lo, lane < hi), tag, 0.0),
                                      axis=1, keepdims=True)
        gate1 = pick(XS_G1, XS_G1 + N_SPLIT)
        gate2 = pick(XS_G2, XS_G2 + N_SPLIT)
        first = pick(XS_E1, XS_E1 + 1) == be_ref[i].astype(F32)
        g = _dot(xb, wg_scr[...])
        up = _dot(xb, wu_scr[...])
        hid = (g * jax.nn.sigmoid(g) * up).astype(BF16)
        y_ref[...] = (jnp.where(first, gate1, gate2) * _dot(hid, wd_scr[...])).astype(y_ref.dtype)

    @pl.when(jnp.logical_not(active))
    def _():
        y_ref[...] = jnp.zeros_like(y_ref)


def _expert_mlp(xs, chunk_src, block_expert, n_used, w_gate, w_up, w_down, layer):
    nb = chunk_src.shape[0]
    R = MOE_ROWS
    wsel = lambda i, be, nu: (layer, be[i], 0, 0)
    idx_spec = lambda f: pl.BlockSpec((1, 1, MOE_CHUNKS), f, memory_space=pltpu.SMEM)
    return pl.pallas_call(
        _expert_kernel,
        out_shape=jax.ShapeDtypeStruct((nb * R, D_MODEL), BF16),
        grid_spec=pltpu.PrefetchScalarGridSpec(
            num_scalar_prefetch=2,
            grid=(nb,),
            in_specs=[idx_spec(lambda i, be, nu: (i, 0, 0)),
                      idx_spec(lambda i, be, nu: (jnp.minimum(i + 1, nb - 1), 0, 0)),
                      idx_spec(lambda i, be, nu: (jnp.minimum(i + 2, nb - 1), 0, 0)),
                      pl.BlockSpec(memory_space=pl.ANY),
                      pl.BlockSpec((None, None, D_MODEL, D_EXPERT), wsel),
                      pl.BlockSpec((None, None, D_MODEL, D_EXPERT), wsel),
                      pl.BlockSpec((None, None, D_EXPERT, D_MODEL), wsel)],
            out_specs=pl.BlockSpec((R, D_MODEL), lambda i, be, nu: (i, 0)),
            scratch_shapes=[pltpu.VMEM((EXPERT_BUFFERS, R, XS_WIDTH), BF16),
                            pltpu.VMEM((D_MODEL, D_EXPERT), BF16),
                            pltpu.VMEM((D_MODEL, D_EXPERT), BF16),
                            pltpu.VMEM((D_EXPERT, D_MODEL), BF16),
                            pltpu.SemaphoreType.DMA((EXPERT_BUFFERS,))]),
        compiler_params=_params(("arbitrary",)),
        name="moe_experts",
    )(block_expert, n_used, chunk_src, chunk_src, chunk_src, xs, w_gate, w_up, w_down)


def _combine_kernel(src0_ref, src_next_ref, ys_hbm, rt_ref, x1_ref, g_ref, b_ref, o_ref, ybuf, sem):
    i = pl.program_id(0)
    nt = pl.num_programs(0)
    slot = lax.rem(i, 2)
    th = x1_ref.shape[0]
    n_local = ybuf.shape[1]
    n_chunks = n_local // MOE_ALIGN

    @pl.when(i == 0)
    def _():
        _chunk_gather(ys_hbm, src0_ref, ybuf.at[0], sem.at[0], n_chunks)

    @pl.when(i + 1 < nt)
    def _():
        _chunk_gather(ys_hbm, src_next_ref, ybuf.at[1 - slot], sem.at[1 - slot], n_chunks)

    _chunk_gather_wait(ys_hbm, ybuf.at[slot], sem.at[slot])
    rt = rt_ref[...]
    pos1 = rt[:, RT_P1:RT_P1 + 1].astype(jnp.int32)
    pos2 = rt[:, RT_P2:RT_P2 + 1].astype(jnp.int32)
    col = lax.broadcasted_iota(jnp.int32, (th, n_local), 1)
    two_hot = jnp.where(jnp.logical_or(col == pos1, col == pos2), 1.0, 0.0).astype(BF16)
    ffn = _dot(two_hot, ybuf[slot])
    o_ref[...] = _layer_norm(DEEPNORM_ALPHA * x1_ref[...] + ffn, g_ref[...], b_ref[...])


def _combine(ys, chunk_src, route, x1, g, b, th):
    T, D = x1.shape
    nt = T // th
    n_chunks = chunk_src.shape[-1]
    idx_spec = lambda f: pl.BlockSpec((1, 1, n_chunks), f, memory_space=pltpu.SMEM)
    return pl.pallas_call(
        _combine_kernel,
        out_shape=jax.ShapeDtypeStruct((T, D), F32),
        grid=(nt,),
        in_specs=[idx_spec(lambda i: (0, 0, 0)),
                  idx_spec(lambda i: (jnp.minimum(i + 1, nt - 1), 0, 0)),
                  pl.BlockSpec(memory_space=pl.ANY),
                  pl.BlockSpec((th, RT_WIDTH), lambda i: (i, 0)),
                  pl.BlockSpec((th, D), lambda i: (i, 0)),
                  pl.BlockSpec((1, D), lambda i: (0, 0)),
                  pl.BlockSpec((1, D), lambda i: (0, 0))],
        out_specs=pl.BlockSpec((th, D), lambda i: (i, 0)),
        scratch_shapes=[pltpu.VMEM((2, n_chunks * MOE_ALIGN, D), BF16), pltpu.SemaphoreType.DMA((2,))],
        compiler_params=_params(("arbitrary",)),
        name="moe_combine",
    )(chunk_src, chunk_src, ys, route, x1, g, b)


def _moe_index_kernel(ch_ref, src_ref, back_ref, be_ref, nu_ref, fill_scr, *, nt, lc, nb):
    i32 = jnp.int32

    def fill(ref, n, val):
        def body(k, carry):
            ref[k] = val
            return carry
        lax.fori_loop(0, n, body, 0, unroll=16)

    fill(src_ref, nb * MOE_CHUNKS, i32(0))
    fill(back_ref, nt * lc, i32(0))
    fill(fill_scr, nt, i32(0))

    def per_expert(e, first_block):
        pos0 = first_block * MOE_CHUNKS

        def per_tile(t, pos):
            n = ch_ref[t * N_EXPERTS + e]
            used = fill_scr[t]
            base = t * lc + used

            def per_chunk(o, carry):
                src_ref[pos + o] = base + o
                back_ref[base + o] = pos + o
                return carry

            lax.fori_loop(0, n, per_chunk, 0)
            fill_scr[t] = used + n
            return pos + n

        pos1 = lax.fori_loop(0, nt, per_tile, pos0)
        n_blocks = (pos1 - pos0 + (MOE_CHUNKS - 1)) // MOE_CHUNKS

        def mark(b, carry):
            be_ref[first_block + b] = e
            return carry

        lax.fori_loop(0, n_blocks, mark, 0)
        return first_block + n_blocks

    n_used = lax.fori_loop(0, N_EXPERTS, per_expert, i32(0))

    def tail(b, carry):
        be_ref[b] = i32(N_EXPERTS - 1)
        return carry

    lax.fori_loop(n_used, nb, tail, 0)
    nu_ref[0] = n_used


def _moe(x1, xs, route, counts, p, layer, tm):
    T = x1.shape[0]
    nt = T // tm
    lc = _local_rows(tm) // MOE_ALIGN
    nb = nt * lc // MOE_CHUNKS + N_EXPERTS
    i32 = jnp.int32
    cnt = counts[:, 0, ROUTER_EXPERT_LANE0:ROUTER_EXPERT_LANE0 + N_EXPERTS].astype(i32)
    ch = ((cnt + MOE_ALIGN - 1) // MOE_ALIGN).reshape(-1)
    smem = pl.BlockSpec(memory_space=pltpu.SMEM)
    src, back, block_expert, n_used = pl.pallas_call(
        functools.partial(_moe_index_kernel, nt=nt, lc=lc, nb=nb),
        out_shape=(jax.ShapeDtypeStruct((nb * MOE_CHUNKS,), i32), jax.ShapeDtypeStruct((nt * lc,), i32),
                   jax.ShapeDtypeStruct((nb,), i32), jax.ShapeDtypeStruct((1,), i32)),
        in_specs=[smem],
        out_specs=(smem, smem, smem, smem),
        scratch_shapes=[pltpu.SMEM((nt,), i32)],
        name="moe_index",
    )(ch)
    ys = _expert_mlp(xs, src.reshape(nb, 1, MOE_CHUNKS), block_expert, n_used,
                     p["w_gate"], p["w_up"], p["w_down"], layer)
    return _combine(ys, back.reshape(nt, 1, lc), route, x1, p["ln2_g"], p["ln2_b"], tm)


def _layer_params(l, w_in, b_in, conv_w, sgu_ln_g, sgu_ln_b, sgu_w, sgu_b, w_branch, w_out, ln1_g, ln1_b,
                  w_group, b_group, w_router, b_router, w_gate, w_up, w_down, ln2_g, ln2_b, tm):
    offs = [0]
    for s in IN_SIZES:
        offs.append(offs[-1] + s)
    wt, b = w_in[l].T, b_in[l]
    seg = lambda a, i: a[offs[i]:offs[i + 1]]
    fox_wt, fox_b = seg(wt, 0), seg(b, 0)
    fq_wt, fk_wt, fv_wt = (fox_wt[j * FOX_WIDTH:(j + 1) * FOX_WIDTH] for j in range(3))
    fq_b, fk_b, fv_b = (fox_b[j * FOX_WIDTH:(j + 1) * FOX_WIDTH] for j in range(3))
    wmt = jnp.concatenate([fk_wt, seg(wt, 2), seg(wt, 3), seg(wt, 4), seg(wt, 6), seg(wt, 7)], axis=0)
    bm = jnp.concatenate([fk_b, seg(b, 2), seg(b, 3), seg(b, 4), seg(b, 6), seg(b, 7)])
    n_small = IN_SIZES[1] + IN_SIZES[5]
    wst = jnp.concatenate([seg(wt, 1), seg(wt, 5)], axis=0).astype(BF16)
    ws = jnp.concatenate([wst, jnp.zeros((LANES - n_small, D_MODEL), BF16)], axis=0).T
    bs = jnp.concatenate([seg(b, 1), seg(b, 5), jnp.zeros((LANES - n_small,), F32)])
    reps = SGU_BLOCK // SGU_CHUNK
    sgu_bias = jnp.broadcast_to(sgu_b[l].T[:, :, None], (SGU_CHUNK, SGU_GROUPS, SGU_WIDTH // SGU_GROUPS))
    sgu_bias = jnp.tile(sgu_bias.reshape(SGU_CHUNK, SGU_WIDTH), (tm // SGU_CHUNK, 1))
    n_route = N_GROUPS + N_EXPERTS
    wr = jnp.concatenate([w_group[l], w_router[l], jnp.zeros((D_MODEL, LANES - n_route), F32)], axis=1)
    br = jnp.concatenate([b_group[l], b_router[l], jnp.zeros((LANES - n_route,), F32)])
    return {
        "wm": wmt.astype(BF16).T, "bm": bm.reshape(1, -1),
        "wqvt": jnp.concatenate([fq_wt, fv_wt], axis=0).astype(BF16),
        "bqvt": jnp.concatenate([fq_b, fv_b]).reshape(-1, 1),
        "ws": ws, "bs": bs.reshape(1, -1),
        "wst": wst, "bst": bs[:SM_ROWS].reshape(-1, 1),
        "conv_w": conv_w[l],
        "sgu_g": sgu_ln_g[l].reshape(1, -1), "sgu_b": sgu_ln_b[l].reshape(1, -1),
        "sgu_wt": jnp.tile(sgu_w[l], (1, reps, reps)), "sgu_bias": sgu_bias,
        "wb": w_branch[l].astype(BF16), "wo": w_out[l].astype(BF16),
        "ln1_g": ln1_g[l].reshape(1, -1), "ln1_b": ln1_b[l].reshape(1, -1),
        "wr": wr.astype(BF16), "br": br.reshape(1, -1),
        "w_gate": w_gate, "w_up": w_up, "w_down": w_down,
        "ln2_g": ln2_g[l].reshape(1, -1), "ln2_b": ln2_b[l].reshape(1, -1),
    }


def _tiles(B, S):
    T = B * S
    tm = 512 if T % 512 == 0 else 256
    tq = 512 if S % 512 == 0 else 256
    return tm, tq


def kernel(x, ln_in_g, ln_in_b, w_in, b_in, conv_w, sgu_ln_g, sgu_ln_b, sgu_w, sgu_b, w_branch, w_out,
           ln1_g, ln1_b, w_group, b_group, w_router, b_router, w_gate, w_up, w_down, ln2_g, ln2_b):
    B, S, D = x.shape
    assert D == D_MODEL and S % SGU_BLOCK == 0
    T = B * S
    tm, tq = _tiles(B, S)
    h = _input_layer_norm(x.reshape(T, D), ln_in_g, ln_in_b, tm)
    for l in range(DEPTH):
        p = _layer_params(l, w_in, b_in, conv_w, sgu_ln_g, sgu_ln_b, sgu_w, sgu_b, w_branch, w_out, ln1_g,
                          ln1_b, w_group, b_group, w_router, b_router, w_gate, w_up, w_down, ln2_g, ln2_b, tm)
        qt, fk, vt, mqk, mv, mo, yc, gates, sm, smt = _in_projection(h, p, tm)
        fcol, frow = _forget_cumsum(sm, smt, B, S)
        ya = _fox_attention(qt, fk, vt, fcol, frow, B, S, tq)
        yb = _mlstm(mqk, mv, mo, sm, smt, p["conv_w"], B, S)
        x1, xs, route, counts = _merge_project_route(ya, yb, yc, gates, h, p, tm)
        h = _moe(x1, xs, route, counts, p, l, tm)
    return h.reshape(B, S, D)
```

```python
import functools

import jax
import jax.numpy as jnp
from jax import lax
from jax.experimental import pallas as pl
from jax.experimental.pallas import tpu as pltpu

F32 = jnp.float32
BF16 = jnp.bfloat16

D_MODEL = 1024
DEPTH = 2
FOX_HEADS = 8
FOX_HEAD_DIM = 64
FOX_WIDTH = FOX_HEADS * FOX_HEAD_DIM
MLSTM_HEADS = 4
MLSTM_HEAD_DIM = 128
MLSTM_WIDTH = MLSTM_HEADS * MLSTM_HEAD_DIM
MLSTM_CHUNK = 128
MLSTM_CONV = 4
SGU_GROUPS = 4
SGU_WIDTH = 512
SGU_CHUNK = 128
N_BRANCH = 3
N_GROUPS = 4
EXPERTS_PER_GROUP = 8
N_EXPERTS = N_GROUPS * EXPERTS_PER_GROUP
TOP_K = 2
D_EXPERT = 512
DEEPNORM_ALPHA = (2 * DEPTH) ** 0.25
LN_EPS = 1e-5
NEG_INF = -1e30

IN_SIZES = (3 * FOX_WIDTH, FOX_HEADS, 2 * MLSTM_WIDTH, MLSTM_WIDTH, MLSTM_WIDTH,
            2 * MLSTM_HEADS, 2 * SGU_WIDTH, N_BRANCH * D_MODEL)

LANES = 128
SUBLANES = 8
VMEM_LIMIT_BYTES = 56 * 1024 * 1024

SM_FOXF = 0
SM_MI = 8
SM_MF = 12
SM_ROWS = 16

RT_E1, RT_E2, RT_P1, RT_P2 = 0, 1, 2, 3
RT_WIDTH = 8
ROUTER_EXPERT_LANE0 = N_GROUPS

MOE_ROWS = 256
MOE_ALIGN = 16
MOE_CHUNKS = MOE_ROWS // MOE_ALIGN
EXPERT_BUFFERS = 3
XS_G1, XS_G2, XS_E1 = 0, 3, 6
XS_WIDTH = D_MODEL + LANES
SGU_BLOCK = 256
CUMSUM_BLOCK = 256


def _params(sem, vmem=VMEM_LIMIT_BYTES):
    return pltpu.CompilerParams(dimension_semantics=sem, vmem_limit_bytes=vmem)


def _resident(shape, index_map):
    return pl.BlockSpec(shape, index_map, pipeline_mode=pl.Buffered(1))


def _layer_norm(x, g, b):
    mu = jnp.mean(x, axis=-1, keepdims=True)
    xc = x - mu
    var = jnp.mean(xc * xc, axis=-1, keepdims=True)
    return xc * lax.rsqrt(var + LN_EPS) * g + b


def _log_sigmoid(x):
    return jnp.minimum(x, 0.0) - jnp.log1p(jnp.exp(-jnp.abs(x)))


def _gelu_tanh(x):
    return 0.5 * x * (1.0 + jnp.tanh(0.7978845608028654 * (x + 0.044715 * x * x * x)))


def _dot(a, b):
    return jnp.dot(a, b, preferred_element_type=F32)


def _dot_nt(a, b):
    return lax.dot_general(a, b, (((1,), (1,)), ((), ())), preferred_element_type=F32)


def _dot_tn(a, b):
    return lax.dot_general(a, b, (((0,), (0,)), ((), ())), preferred_element_type=F32)


def _dot_f32(a, b):
    return jnp.dot(a, b, preferred_element_type=F32, precision=lax.Precision.HIGHEST)


C_FK = 0
C_MQK = C_FK + FOX_WIDTH
C_MV = C_MQK + 2 * MLSTM_WIDTH
C_MO = C_MV + MLSTM_WIDTH
C_SU = C_MO + MLSTM_WIDTH
C_SV = C_SU + SGU_WIDTH
C_GATE = C_SV + SGU_WIDTH
C_END = C_GATE + N_BRANCH * D_MODEL
PROJ_PIECE = 512


def _inproj_kernel(x_ref, lg_ref, lb_ref, wm_ref, bm_ref, wqvt_ref, bqvt_ref, ws_ref, bs_ref, wst_ref, bst_ref,
                   sg_ref, sb_ref, swt_ref, sbias_ref,
                   qt_ref, fk_ref, vt_ref, mqk_ref, mv_ref, mo_ref, yc_ref, gt_ref, sm_ref, smt_ref, h_ref,
                   wbd_scr, *, input_norm):
    tm = x_ref.shape[0]

    @pl.when(pl.program_id(0) == 0)
    def _():
        r = lax.broadcasted_iota(jnp.int32, (SGU_BLOCK, SGU_BLOCK), 0)
        c = lax.broadcasted_iota(jnp.int32, (SGU_BLOCK, SGU_BLOCK), 1)
        keep = jnp.logical_and(r // SGU_CHUNK == c // SGU_CHUNK, r >= c)
        for g in range(SGU_GROUPS):
            wbd_scr[g] = jnp.where(keep, swt_ref[g], 0.0).astype(BF16)

    x = x_ref[...]
    if input_norm:
        x = _layer_norm(x, lg_ref[...], lb_ref[...])
        h_ref[...] = x
    else:
        h_ref[...] = jnp.zeros_like(h_ref)
    xb = x.astype(BF16)

    def proj(lo, width=PROJ_PIECE):
        return _dot(xb, wm_ref[:, lo:lo + width]) + bm_ref[:, lo:lo + width]

    fk_ref[...] = proj(C_FK).astype(BF16)
    qvt = _dot_nt(wqvt_ref[...], xb) + bqvt_ref[...]
    qt_ref[...] = (qvt[:FOX_WIDTH] * (LOG2E * FOX_HEAD_DIM ** -0.5)).astype(BF16)
    vt_ref[...] = qvt[FOX_WIDTH:].astype(BF16)
    for j in range(2):
        mqk_ref[:, j * PROJ_PIECE:(j + 1) * PROJ_PIECE] = proj(C_MQK + j * PROJ_PIECE).astype(BF16)
    mv_ref[...] = proj(C_MV).astype(BF16)
    mo_ref[...] = jax.nn.sigmoid(proj(C_MO)).astype(BF16)
    for j in range(N_BRANCH * D_MODEL // PROJ_PIECE):
        gt_ref[:, j * PROJ_PIECE:(j + 1) * PROJ_PIECE] = jax.nn.sigmoid(
            proj(C_GATE + j * PROJ_PIECE)).astype(BF16)

    sm = _dot(xb, ws_ref[...]) + bs_ref[...]
    lane = lax.broadcasted_iota(jnp.int32, sm.shape, 1)
    is_ls = jnp.logical_or(lane < SM_MI, jnp.logical_and(lane >= SM_MF, lane < SM_ROWS))
    sm_ref[...] = jnp.where(is_ls, _log_sigmoid(sm), sm)
    smt = _dot_nt(wst_ref[...], xb) + bst_ref[...]
    row = lax.broadcasted_iota(jnp.int32, smt.shape, 0)
    is_ls_t = jnp.logical_or(row < SM_MI, row >= SM_MF)
    smt_ref[...] = jnp.where(is_ls_t, _log_sigmoid(smt), smt)

    u = _gelu_tanh(proj(C_SU))
    v = _gelu_tanh(proj(C_SV))
    vn = _layer_norm(v, sg_ref[...], sb_ref[...]).astype(BF16)
    for rb in range(tm // SGU_BLOCK):
        rows = slice(rb * SGU_BLOCK, (rb + 1) * SGU_BLOCK)
        for g in range(SGU_GROUPS):
            cols = slice(g * LANES, (g + 1) * LANES)
            z = _dot(wbd_scr[g], vn[rows, cols]) + sbias_ref[rows, cols]
            yc_ref[rows, cols] = (u[rows, cols] * z).astype(BF16)


def _in_projection(x, ln_g, ln_b, p, tm, input_norm):
    T, D = x.shape
    row = lambda w: pl.BlockSpec((tm, w), lambda i: (i, 0))
    out_shapes = (
        jax.ShapeDtypeStruct((FOX_WIDTH, T), BF16),
        jax.ShapeDtypeStruct((T, FOX_WIDTH), BF16),
        jax.ShapeDtypeStruct((FOX_WIDTH, T), BF16),
        jax.ShapeDtypeStruct((T, 2 * MLSTM_WIDTH), BF16),
        jax.ShapeDtypeStruct((T, MLSTM_WIDTH), BF16),
        jax.ShapeDtypeStruct((T, MLSTM_WIDTH), BF16),
        jax.ShapeDtypeStruct((T, SGU_WIDTH), BF16),
        jax.ShapeDtypeStruct((T, N_BRANCH * D_MODEL), BF16),
        jax.ShapeDtypeStruct((T, LANES), F32),
        jax.ShapeDtypeStruct((SM_ROWS, T), F32),
        jax.ShapeDtypeStruct((T, D) if input_norm else (SUBLANES, LANES), F32),
    )
    col = lambda h: pl.BlockSpec((h, tm), lambda i: (0, i))
    out_specs = (col(FOX_WIDTH), row(FOX_WIDTH), col(FOX_WIDTH),
                 row(2 * MLSTM_WIDTH), row(MLSTM_WIDTH), row(MLSTM_WIDTH), row(SGU_WIDTH),
                 row(N_BRANCH * D_MODEL), row(LANES), col(SM_ROWS),
                 row(D) if input_norm else pl.BlockSpec((SUBLANES, LANES), lambda i: (0, 0)))
    const = lambda shape: _resident(shape, lambda i: tuple(0 for _ in shape))
    in_specs = [
        row(D), const((1, D)), const((1, D)),
        const((D, C_END)), const((1, C_END)),
        const((2 * FOX_WIDTH, D)), const((2 * FOX_WIDTH, 1)),
        const((D, LANES)), const((1, LANES)),
        const((SM_ROWS, D)), const((SM_ROWS, 1)),
        const((1, SGU_WIDTH)), const((1, SGU_WIDTH)),
        const((SGU_GROUPS, SGU_BLOCK, SGU_BLOCK)),
        const((tm, SGU_WIDTH)),
    ]
    return pl.pallas_call(
        functools.partial(_inproj_kernel, input_norm=input_norm),
        out_shape=out_shapes,
        grid=(T // tm,),
        in_specs=in_specs,
        out_specs=out_specs,
        scratch_shapes=[pltpu.VMEM((SGU_GROUPS, SGU_BLOCK, SGU_BLOCK), BF16)],
        compiler_params=_params(("arbitrary",)),
        name="in_proj",
    )(x, ln_g.reshape(1, D), ln_b.reshape(1, D), p["wm"], p["bm"], p["wqvt"], p["bqvt"], p["ws"], p["bs"], p["wst"], p["bst"],
      p["sgu_g"], p["sgu_b"], p["sgu_wt"], p["sgu_bias"])


def _fcum_kernel(sm_ref, smt_ref, fcol_ref, frow_ref):
    S = sm_ref.shape[0]
    cb = min(CUMSUM_BLOCK, S)
    r = lax.broadcasted_iota(jnp.int32, (cb, cb), 0)
    c = lax.broadcasted_iota(jnp.int32, (cb, cb), 1)
    lower = (r >= c).astype(F32)
    upper = (r <= c).astype(F32)
    carry_c = jnp.zeros((1, LANES), F32)
    carry_r = jnp.zeros((SM_ROWS, 1), F32)
    for blk in range(S // cb):
        sl = slice(blk * cb, (blk + 1) * cb)
        fc = _dot_f32(lower, sm_ref[sl, :]) + carry_c
        fcol_ref[sl, :] = fc
        carry_c = fc[cb - 1:cb, :]
        fr = _dot_f32(smt_ref[:, sl], upper) + carry_r
        frow_ref[:, sl] = fr
        carry_r = fr[:, cb - 1:cb]


def _forget_cumsum(sm, smt, B, S):
    T = B * S
    return pl.pallas_call(
        _fcum_kernel,
        out_shape=(jax.ShapeDtypeStruct((T, LANES), F32), jax.ShapeDtypeStruct((SM_ROWS, T), F32)),
        grid=(B,),
        in_specs=[pl.BlockSpec((S, LANES), lambda b: (b, 0)),
                  pl.BlockSpec((SM_ROWS, S), lambda b: (0, b))],
        out_specs=(pl.BlockSpec((S, LANES), lambda b: (b, 0)),
                   pl.BlockSpec((SM_ROWS, S), lambda b: (0, b))),
        compiler_params=_params(("parallel",)),
        name="forget_cumsum",
    )(sm, smt)


N_SPLIT = 3
LOG2E = 1.4426950408889634


def _split3(f):
    hi = f.astype(BF16).astype(F32)
    r1 = f - hi
    mid = r1.astype(BF16).astype(F32)
    lo = r1 - mid
    return hi, mid, lo


def _fox_kernel(qt_ref, k_ref, vt_ref, fcol_ref, frow_ref, o_ref, kaug_scr, s_scr, m_scr, l_scr, acc_scr, *, tq):
    S = k_ref.shape[0]
    nq = S // tq
    pair = pl.program_id(1)

    lane = lax.broadcasted_iota(jnp.int32, (S, LANES), 1)
    fc = fcol_ref[...]
    k = k_ref[...]
    for hh in range(2):
        f = jnp.sum(jnp.where(lane == 2 * pair + hh, fc, 0.0), axis=1, keepdims=True)
        hi, mid, lo = _split3(LOG2E * f)
        aug = jnp.where(lane < N_SPLIT, 1.0,
                        jnp.where(lane == N_SPLIT, -hi,
                                  jnp.where(lane == N_SPLIT + 1, -mid,
                                            jnp.where(lane == N_SPLIT + 2, -lo, 0.0))))
        kaug_scr[hh, :, 0:LANES] = k
        kaug_scr[hh, :, LANES:2 * LANES] = aug.astype(BF16)

    sub = lax.broadcasted_iota(jnp.int32, (LANES, tq), 0)
    key = lax.broadcasted_iota(jnp.int32, (tq, tq), 0)
    qry = lax.broadcasted_iota(jnp.int32, (tq, tq), 1)
    causal = key <= qry

    def start_tile(qi):
        cols = slice(qi * tq, (qi + 1) * tq)
        qt = qt_ref[:, cols]
        rhs = []
        for hh in range(2):
            hi, mid, lo = _split3(LOG2E * frow_ref[hh][:, cols])
            augq = jnp.where(sub == 0, hi,
                             jnp.where(sub == 1, mid,
                                       jnp.where(sub == 2, lo,
                                                 jnp.where(sub < 2 * N_SPLIT, 1.0, 0.0))))
            in_head = (sub < FOX_HEAD_DIM) if hh == 0 else (sub >= FOX_HEAD_DIM)
            qm = jnp.where(in_head, qt, jnp.zeros_like(qt))
            rhs.append(jnp.concatenate([qm, augq.astype(BF16)], axis=0))
            m_scr[qi % 2, hh] = jnp.full((1, tq), NEG_INF, F32)
            l_scr[qi % 2, hh] = jnp.zeros((1, tq), F32)
            acc_scr[qi % 2, hh] = jnp.zeros((LANES, tq), F32)
        return rhs

    def scores(rhs, j, hh, slot):
        s_scr[slot, hh] = _dot(kaug_scr[hh, j * tq:(j + 1) * tq, :], rhs[hh])

    def consume(qi, j, hh, slot):
        par = qi % 2
        s = s_scr[slot, hh]
        if j == qi:
            s = jnp.where(causal, s, NEG_INF)
        m = m_scr[par, hh]
        m_new = jnp.maximum(m, jnp.max(s, axis=0, keepdims=True))
        p = jnp.exp2(s - m_new)
        corr = jnp.exp2(m - m_new)
        m_scr[par, hh] = m_new
        l_scr[par, hh] = l_scr[par, hh] * corr + jnp.sum(p, axis=0, keepdims=True)
        acc_scr[par, hh] = acc_scr[par, hh] * corr + _dot(vt_ref[:, j * tq:(j + 1) * tq], p.astype(BF16))

    steps = [(qi, j) for qi in range(nq) for j in range(qi + 1)]
    rhs = start_tile(0)
    for hh in range(2):
        scores(rhs, 0, hh, 0)
    for t, (qi, j) in enumerate(steps):
        slot = t % 2
        nxt = steps[t + 1] if t + 1 < len(steps) else None
        rhs_next = rhs
        if nxt is not None and nxt[1] == 0:
            rhs_next = start_tile(nxt[0])
        for hh in range(2):
            if nxt is not None:
                scores(rhs_next, nxt[1], hh, 1 - slot)
            consume(qi, j, hh, slot)
        if j == qi:
            outs = [acc_scr[qi % 2, hh] / l_scr[qi % 2, hh] for hh in range(2)]
            out_t = jnp.where(sub < FOX_HEAD_DIM, outs[0], outs[1])
            o_ref[qi * tq:(qi + 1) * tq, :] = out_t.T.astype(o_ref.dtype)
        rhs = rhs_next


def _fox_attention(qt, fk, vt, fcol, frow, B, S, tq):
    T = B * S
    n_pairs = FOX_HEADS // 2
    frow3 = frow.reshape(SM_ROWS, 1, T)
    return pl.pallas_call(
        functools.partial(_fox_kernel, tq=tq),
        out_shape=jax.ShapeDtypeStruct((T, FOX_WIDTH), BF16),
        grid=(B, n_pairs),
        in_specs=[pl.BlockSpec((LANES, S), lambda b, p: (p, b)),
                  pl.BlockSpec((S, LANES), lambda b, p: (b, p)),
                  pl.BlockSpec((LANES, S), lambda b, p: (p, b)),
                  pl.BlockSpec((S, LANES), lambda b, p: (b, 0)),
                  pl.BlockSpec((2, 1, S), lambda b, p: (p, 0, b))],
        out_specs=pl.BlockSpec((S, LANES), lambda b, p: (b, p)),
        scratch_shapes=[pltpu.VMEM((2, S, 2 * LANES), BF16),
                        pltpu.VMEM((2, 2, tq, tq), F32),
                        pltpu.VMEM((2, 2, 1, tq), F32),
                        pltpu.VMEM((2, 2, 1, tq), F32),
                        pltpu.VMEM((2, 2, LANES, tq), F32)],
        compiler_params=_params(("parallel", "parallel")),
        name="fox_attention",
    )(qt, fk, vt, fcol, frow3)


def _mlstm_kernel(qk_ref, v_ref, og_ref, sm_ref, smt_ref, cw_ref, y_ref, c_scr, n_scr, m_scr, tail_scr):
    L = MLSTM_CHUNK
    dh = MLSTM_HEAD_DIM

    @pl.when(pl.program_id(1) == 0)
    def _():
        c_scr[...] = jnp.zeros_like(c_scr)
        n_scr[...] = jnp.zeros_like(n_scr)
        m_scr[...] = jnp.zeros_like(m_scr)
        tail_scr[...] = jnp.zeros_like(tail_scr)

    x = qk_ref[...].astype(F32)
    xe = jnp.concatenate([tail_scr[...], x], axis=0)
    cw = cw_ref[...]
    y = x * cw[MLSTM_CONV - 1:MLSTM_CONV, :]
    for k in range(1, MLSTM_CONV):
        shifted = pltpu.roll(xe, k, 0)[SUBLANES:, :]
        y = y + shifted * cw[MLSTM_CONV - 1 - k:MLSTM_CONV - k, :]
    tail_scr[...] = x[L - SUBLANES:, :]
    y = y * jax.nn.sigmoid(y)
    q_all = y[:, :MLSTM_WIDTH]
    k_all = y[:, MLSTM_WIDTH:] * (dh ** -0.5)

    sm = sm_ref[...]
    smt = smt_ref[...]
    r = lax.broadcasted_iota(jnp.int32, (L, L), 0)
    c = lax.broadcasted_iota(jnp.int32, (L, L), 1)
    causal = r >= c
    bcol_all = _dot_f32(causal.astype(F32), sm)
    brow_all = _dot_f32(smt, (r <= c).astype(F32))

    heads = range(MLSTM_HEADS)
    hsl = [slice(h * dh, (h + 1) * dh) for h in heads]
    qh = [q_all[:, hsl[h]] for h in heads]
    qb = [qh[h].astype(BF16) for h in heads]
    kh = [k_all[:, hsl[h]] for h in heads]
    vh = [v_ref[:, hsl[h]] for h in heads]
    c_prev = [c_scr[h] for h in heads]
    n_prev = [n_scr[h:h + 1, :] for h in heads]
    m_prev = [m_scr[h:h + 1, 0:1] for h in heads]
    s_qk = [_dot_nt(qb[h], kh[h].astype(BF16)) for h in heads]
    q_c = [_dot(qb[h], c_prev[h].astype(BF16)) for h in heads]

    bq = [bcol_all[:, SM_MF + h:SM_MF + h + 1] for h in heads]
    li_c = [sm[:, SM_MI + h:SM_MI + h + 1] for h in heads]
    br = [brow_all[SM_MF + h:SM_MF + h + 1, :] for h in heads]
    li_r = [smt[SM_MI + h:SM_MI + h + 1, :] for h in heads]
    b_last = [bq[h][L - 1:L, :] for h in heads]
    d = [jnp.where(causal, bq[h] - br[h] + li_r[h], NEG_INF) for h in heads]
    inter = [bq[h] + m_prev[h] for h in heads]
    m_t = [jnp.maximum(inter[h], jnp.max(d[h], axis=1, keepdims=True)) for h in heads]
    w_intra = [jnp.exp(d[h] - m_t[h]) for h in heads]
    w_inter = [jnp.exp(inter[h] - m_t[h]) for h in heads]

    qk = [s_qk[h] * w_intra[h] for h in heads]
    num = [_dot(qk[h].astype(BF16), vh[h]) + w_inter[h] * q_c[h] for h in heads]
    den = [jnp.sum(qk[h], axis=1, keepdims=True)
           + w_inter[h] * jnp.sum(qh[h] * n_prev[h], axis=1, keepdims=True) for h in heads]
    for h in heads:
        h_c = num[h] / jnp.maximum(jnp.abs(den[h]), jnp.exp(-m_t[h]))
        y_ref[:, hsl[h]] = (og_ref[:, hsl[h]].astype(F32) * h_c).astype(y_ref.dtype)

    g_c = [b_last[h] - bq[h] + li_c[h] for h in heads]
    m_new = [jnp.maximum(b_last[h] + m_prev[h], jnp.max(g_c[h], axis=0, keepdims=True)) for h in heads]
    decay = [jnp.exp(b_last[h] + m_prev[h] - m_new[h]) for h in heads]
    kw = [kh[h] * jnp.exp(g_c[h] - m_new[h]) for h in heads]
    for h in heads:
        c_scr[h] = decay[h] * c_prev[h] + _dot_tn(kw[h].astype(BF16), vh[h])
        n_scr[h:h + 1, :] = decay[h] * n_prev[h] + jnp.sum(kw[h], axis=0, keepdims=True)
        m_scr[h:h + 1, :] = jnp.broadcast_to(m_new[h], (1, LANES))


def _mlstm(mqk, mv, mo, sm, smt, conv_w, B, S):
    T = B * S
    L = MLSTM_CHUNK
    nc = S // L
    row = lambda w: pl.BlockSpec((L, w), lambda b, c: (b * nc + c, 0))
    return pl.pallas_call(
        _mlstm_kernel,
        out_shape=jax.ShapeDtypeStruct((T, MLSTM_WIDTH), BF16),
        grid=(B, nc),
        in_specs=[row(2 * MLSTM_WIDTH), row(MLSTM_WIDTH), row(MLSTM_WIDTH), row(LANES),
                  pl.BlockSpec((SM_ROWS, L), lambda b, c: (0, b * nc + c)),
                  pl.BlockSpec((MLSTM_CONV, 2 * MLSTM_WIDTH), lambda b, c: (0, 0))],
        out_specs=row(MLSTM_WIDTH),
        scratch_shapes=[pltpu.VMEM((MLSTM_HEADS, MLSTM_HEAD_DIM, MLSTM_HEAD_DIM), F32),
                        pltpu.VMEM((SUBLANES, MLSTM_HEAD_DIM), F32),
                        pltpu.VMEM((SUBLANES, LANES), F32),
                        pltpu.VMEM((SUBLANES, 2 * MLSTM_WIDTH), F32)],
        compiler_params=_params(("parallel", "arbitrary")),
        name="mlstm",
    )(mqk, mv, mo, sm, smt, conv_w)


def _merge_kernel(ya_ref, yb_ref, yc_ref, gt_ref, x_ref, wb_ref, wo_ref, g_ref, b_ref, wr_ref, br_ref,
                  x1_ref, xs_ref, rt_ref, cnt_ref, x1_scr):
    tm = x_ref.shape[0]

    @pl.when(pl.program_id(0) == 0)
    def _():
        x1_scr[...] = jnp.zeros_like(x1_scr)

    xp = x1_scr[...]
    xpb = xp.astype(BF16)
    lane = lax.broadcasted_iota(jnp.int32, (tm, LANES), 1)
    big = jnp.int32(LANES)

    def branch(n, y_ref):
        return _dot(y_ref[...], wb_ref[n]) * gt_ref[:, n * D_MODEL:(n + 1) * D_MODEL].astype(F32)

    merged = branch(0, ya_ref)

    logits = _dot(xpb, wr_ref[...]) + br_ref[...]
    glog = jnp.where(lane < N_GROUPS, logits, -jnp.inf)
    gmax = jnp.max(glog, axis=1, keepdims=True)
    g_top = jnp.min(jnp.where(glog == gmax, lane, big), axis=1, keepdims=True)
    p_g = 1.0 / jnp.sum(jnp.exp(glog - gmax), axis=1, keepdims=True)

    merged = merged + branch(1, yb_ref)

    lo = ROUTER_EXPERT_LANE0 + EXPERTS_PER_GROUP * g_top
    el = jnp.where(jnp.logical_and(lane >= lo, lane < lo + EXPERTS_PER_GROUP), logits, -jnp.inf)
    m1 = jnp.max(el, axis=1, keepdims=True)
    i1 = jnp.min(jnp.where(el == m1, lane, big), axis=1, keepdims=True)
    el2 = jnp.where(lane == i1, -jnp.inf, el)
    m2 = jnp.max(el2, axis=1, keepdims=True)
    i2 = jnp.min(jnp.where(el2 == m2, lane, big), axis=1, keepdims=True)
    ratio = jnp.exp(m2 - m1)
    gate1 = p_g / (1.0 + ratio)
    gate2 = p_g * ratio / (1.0 + ratio)

    merged = merged + branch(2, yc_ref)

    hit1 = lane == i1
    hit2 = lane == i2
    onehot = jnp.where(jnp.logical_or(hit1, hit2), 1.0, 0.0)
    r = lax.broadcasted_iota(jnp.int32, (tm, tm), 0)
    c = lax.broadcasted_iota(jnp.int32, (tm, tm), 1)
    before = jnp.where(r > c, 1.0, 0.0).astype(BF16)
    seen = _dot(before, onehot.astype(BF16))
    count = jnp.sum(onehot, axis=0, keepdims=True)
    chunks = jnp.floor((count + (MOE_ALIGN - 1)) * (1.0 / MOE_ALIGN))
    er = lax.broadcasted_iota(jnp.int32, (LANES, LANES), 0)
    ec = lax.broadcasted_iota(jnp.int32, (LANES, LANES), 1)
    earlier = jnp.where(er < ec, 1.0, 0.0)
    start = MOE_ALIGN * _dot_f32(jnp.broadcast_to(chunks, (SUBLANES, LANES)), earlier)[0:1, :]
    where_to = start + seen
    pos1 = jnp.sum(jnp.where(hit1, where_to, 0.0), axis=1, keepdims=True)
    pos2 = jnp.sum(jnp.where(hit2, where_to, 0.0), axis=1, keepdims=True)
    cnt_ref[0] = jnp.broadcast_to(count, (SUBLANES, LANES))

    e1 = (i1 - ROUTER_EXPERT_LANE0).astype(F32)
    e2 = (i2 - ROUTER_EXPERT_LANE0).astype(F32)
    rec = jnp.zeros((tm, LANES), F32)
    for pos, val in ((RT_E1, e1), (RT_E2, e2), (RT_P1, pos1), (RT_P2, pos2)):
        rec = jnp.where(lane == pos, val, rec)
    rt_ref[...] = rec[:, :RT_WIDTH]

    mix = _dot(merged.astype(BF16), wo_ref[...])

    tag = jnp.zeros((tm, LANES), F32)
    pieces = _split3(gate1) + _split3(gate2) + (e1,)
    for pos, val in enumerate(pieces):
        tag = jnp.where(lane == pos, val, tag)
    xa = jnp.concatenate([xpb, tag.astype(BF16)], axis=1)
    n_local = xs_ref.shape[0]
    col = lax.broadcasted_iota(jnp.int32, (tm, n_local), 1)
    sel = jnp.logical_or(col == pos1.astype(jnp.int32), col == pos2.astype(jnp.int32))
    xs_ref[...] = _dot_tn(jnp.where(sel, 1.0, 0.0).astype(BF16), xa).astype(BF16)

    x1 = _layer_norm(DEEPNORM_ALPHA * x_ref[...] + mix, g_ref[...], b_ref[...])
    x1_ref[...] = x1
    x1_scr[...] = x1


def _local_rows(tm):
    worst = TOP_K * tm + N_EXPERTS * (MOE_ALIGN - 1)
    return -(-worst // LANES) * LANES


def _merge_project_route(ya, yb, yc, gates, x, p, tm):
    T, D = x.shape
    nt = T // tm
    n_local = _local_rows(tm)
    cur = lambda i: jnp.minimum(i, nt - 1)
    prev = lambda i: jnp.maximum(i - 1, 0)
    row = lambda w: pl.BlockSpec((tm, w), lambda i: (cur(i), 0))
    const = lambda shape: _resident(shape, lambda i: tuple(0 for _ in shape))
    return pl.pallas_call(
        _merge_kernel,
        out_shape=(jax.ShapeDtypeStruct((T, D), F32),
                   jax.ShapeDtypeStruct((nt * n_local, XS_WIDTH), BF16),
                   jax.ShapeDtypeStruct((T, RT_WIDTH), F32),
                   jax.ShapeDtypeStruct((nt, SUBLANES, LANES), F32)),
        grid=(nt + 1,),
        in_specs=[row(FOX_WIDTH), row(MLSTM_WIDTH), row(SGU_WIDTH), row(N_BRANCH * D), row(D),
                  const((N_BRANCH, FOX_WIDTH, D)), const((D, D)), const((1, D)), const((1, D)),
                  const((D, LANES)), const((1, LANES))],
        out_specs=(row(D), pl.BlockSpec((n_local, XS_WIDTH), lambda i: (prev(i), 0)),
                   pl.BlockSpec((tm, RT_WIDTH), lambda i: (prev(i), 0)),
                   pl.BlockSpec((1, SUBLANES, LANES), lambda i: (prev(i), 0, 0))),
        scratch_shapes=[pltpu.VMEM((tm, D), F32)],
        compiler_params=_params(("arbitrary",)),
        name="merge_route",
    )(ya, yb, yc, gates, x, p["wb"], p["wo"], p["ln1_g"], p["ln1_b"], p["wr"], p["br"])


def _chunk_gather(src_hbm, idx_ref, dst, sem, n_chunks):
    for j in range(n_chunks):
        start = pl.multiple_of(idx_ref[0, 0, j] * MOE_ALIGN, MOE_ALIGN)
        pltpu.make_async_copy(src_hbm.at[pl.ds(start, MOE_ALIGN)], dst.at[pl.ds(j * MOE_ALIGN, MOE_ALIGN)],
                              sem).start(priority=j % 2)


def _chunk_gather_wait(src_hbm, dst, sem):
    n = dst.shape[0]
    pltpu.make_async_copy(src_hbm.at[pl.ds(0, n)], dst, sem).wait()


def _expert_kernel(be_ref, nu_ref, src0_ref, src1_ref, src2_ref, xs_hbm, wg_ref, wu_ref, wd_ref, y_ref,
                   xbuf, wg_scr, wu_scr, wd_scr, gsem):
    i = pl.program_id(0)
    n_used = nu_ref[0]
    slot = lax.rem(i, EXPERT_BUFFERS)
    active = i < n_used

    for ahead, idx_ref in ((0, src0_ref), (1, src1_ref)):
        @pl.when(jnp.logical_and(i == 0, ahead < n_used))
        def _(ahead=ahead, idx_ref=idx_ref):
            _chunk_gather(xs_hbm, idx_ref, xbuf.at[ahead], gsem.at[ahead], MOE_CHUNKS)

    @pl.when(i + 2 < n_used)
    def _():
        nxt = lax.rem(i + 2, EXPERT_BUFFERS)
        _chunk_gather(xs_hbm, src2_ref, xbuf.at[nxt], gsem.at[nxt], MOE_CHUNKS)

    new_expert = jnp.logical_or(i == 0, be_ref[i] != be_ref[jnp.maximum(i - 1, 0)])

    @pl.when(jnp.logical_and(active, new_expert))
    def _():
        wg_scr[...] = wg_ref[...].astype(BF16)
        wu_scr[...] = wu_ref[...].astype(BF16)
        wd_scr[...] = wd_ref[...].astype(BF16)

    @pl.when(active)
    def _():
        _chunk_gather_wait(xs_hbm, xbuf.at[slot], gsem.at[slot])
        xa = xbuf[slot]
        xb = xa[:, :D_MODEL]
        tag = xa[:, D_MODEL:].astype(F32)
        lane = lax.broadcasted_iota(jnp.int32, tag.shape, 1)
        pick = lambda lo, hi: jnp.sum(jnp.where(jnp.logical_and(lane >= lo, lane < hi), tag, 0.0),
                                      axis=1, keepdims=True)
        gate1 = pick(XS_G1, XS_G1 + N_SPLIT)
        gate2 = pick(XS_G2, XS_G2 + N_SPLIT)
        first = pick(XS_E1, XS_E1 + 1) == be_ref[i].astype(F32)
        g = _dot(xb, wg_scr[...])
        up = _dot(xb, wu_scr[...])
        hid = (g * jax.nn.sigmoid(g) * up).astype(BF16)
        y_ref[...] = (jnp.where(first, gate1, gate2) * _dot(hid, wd_scr[...])).astype(y_ref.dtype)

    @pl.when(jnp.logical_not(active))
    def _():
        y_ref[...] = jnp.zeros_like(y_ref)


def _expert_mlp(xs, chunk_src, block_expert, n_used, w_gate, w_up, w_down, layer):
    nb = chunk_src.shape[0]
    R = MOE_ROWS
    wsel = lambda i, be, nu: (layer, be[i], 0, 0)
    idx_spec = lambda f: pl.BlockSpec((1, 1, MOE_CHUNKS), f, memory_space=pltpu.SMEM)
    return pl.pallas_call(
        _expert_kernel,
        out_shape=jax.ShapeDtypeStruct((nb * R, D_MODEL), BF16),
        grid_spec=pltpu.PrefetchScalarGridSpec(
            num_scalar_prefetch=2,
            grid=(nb,),
            in_specs=[idx_spec(lambda i, be, nu: (i, 0, 0)),
                      idx_spec(lambda i, be, nu: (jnp.minimum(i + 1, nb - 1), 0, 0)),
                      idx_spec(lambda i, be, nu: (jnp.minimum(i + 2, nb - 1), 0, 0)),
                      pl.BlockSpec(memory_space=pl.ANY),
                      pl.BlockSpec((None, None, D_MODEL, D_EXPERT), wsel),
                      pl.BlockSpec((None, None, D_MODEL, D_EXPERT), wsel),
                      pl.BlockSpec((None, None, D_EXPERT, D_MODEL), wsel)],
            out_specs=pl.BlockSpec((R, D_MODEL), lambda i, be, nu: (i, 0)),
            scratch_shapes=[pltpu.VMEM((EXPERT_BUFFERS, R, XS_WIDTH), BF16),
                            pltpu.VMEM((D_MODEL, D_EXPERT), BF16),
                            pltpu.VMEM((D_MODEL, D_EXPERT), BF16),
                            pltpu.VMEM((D_EXPERT, D_MODEL), BF16),
                            pltpu.SemaphoreType.DMA((EXPERT_BUFFERS,))]),
        compiler_params=_params(("arbitrary",)),
        name="moe_experts",
    )(block_expert, n_used, chunk_src, chunk_src, chunk_src, xs, w_gate, w_up, w_down)


def _combine_kernel(src0_ref, src_next_ref, ys_hbm, rt_ref, x1_ref, g_ref, b_ref, o_ref, ybuf, sem):
    i = pl.program_id(0)
    nt = pl.num_programs(0)
    slot = lax.rem(i, 2)
    th = x1_ref.shape[0]
    n_local = ybuf.shape[1]
    n_chunks = n_local // MOE_ALIGN

    @pl.when(i == 0)
    def _():
        _chunk_gather(ys_hbm, src0_ref, ybuf.at[0], sem.at[0], n_chunks)

    @pl.when(i + 1 < nt)
    def _():
        _chunk_gather(ys_hbm, src_next_ref, ybuf.at[1 - slot], sem.at[1 - slot], n_chunks)

    _chunk_gather_wait(ys_hbm, ybuf.at[slot], sem.at[slot])
    rt = rt_ref[...]
    pos1 = rt[:, RT_P1:RT_P1 + 1].astype(jnp.int32)
    pos2 = rt[:, RT_P2:RT_P2 + 1].astype(jnp.int32)
    col = lax.broadcasted_iota(jnp.int32, (th, n_local), 1)
    two_hot = jnp.where(jnp.logical_or(col == pos1, col == pos2), 1.0, 0.0).astype(BF16)
    ffn = _dot(two_hot, ybuf[slot])
    o_ref[...] = _layer_norm(DEEPNORM_ALPHA * x1_ref[...] + ffn, g_ref[...], b_ref[...])


def _combine(ys, chunk_src, route, x1, g, b, th):
    T, D = x1.shape
    nt = T // th
    n_chunks = chunk_src.shape[-1]
    idx_spec = lambda f: pl.BlockSpec((1, 1, n_chunks), f, memory_space=pltpu.SMEM)
    return pl.pallas_call(
        _combine_kernel,
        out_shape=jax.ShapeDtypeStruct((T, D), F32),
        grid=(nt,),
        in_specs=[idx_spec(lambda i: (0, 0, 0)),
                  idx_spec(lambda i: (jnp.minimum(i + 1, nt - 1), 0, 0)),
                  pl.BlockSpec(memory_space=pl.ANY),
                  pl.BlockSpec((th, RT_WIDTH), lambda i: (i, 0)),
                  pl.BlockSpec((th, D), lambda i: (i, 0)),
                  pl.BlockSpec((1, D), lambda i: (0, 0)),
                  pl.BlockSpec((1, D), lambda i: (0, 0))],
        out_specs=pl.BlockSpec((th, D), lambda i: (i, 0)),
        scratch_shapes=[pltpu.VMEM((2, n_chunks * MOE_ALIGN, D), BF16), pltpu.SemaphoreType.DMA((2,))],
        compiler_params=_params(("arbitrary",)),
        name="moe_combine",
    )(chunk_src, chunk_src, ys, route, x1, g, b)


def _moe_index_kernel(ch_ref, src_ref, back_ref, be_ref, nu_ref, fill_scr, *, nt, lc, nb):
    i32 = jnp.int32

    def fill(ref, n, val):
        def body(k, carry):
            ref[k] = val
            return carry
        lax.fori_loop(0, n, body, 0, unroll=16)

    fill(src_ref, nb * MOE_CHUNKS, i32(0))
    fill(back_ref, nt * lc, i32(0))
    fill(fill_scr, nt, i32(0))

    def per_expert(e, first_block):
        pos0 = first_block * MOE_CHUNKS

        def per_tile(t, pos):
            n = ch_ref[t * N_EXPERTS + e]
            used = fill_scr[t]
            base = t * lc + used

            def per_chunk(o, carry):
                src_ref[pos + o] = base + o
                back_ref[base + o] = pos + o
                return carry

            lax.fori_loop(0, n, per_chunk, 0)
            fill_scr[t] = used + n
            return pos + n

        pos1 = lax.fori_loop(0, nt, per_tile, pos0)
        n_blocks = (pos1 - pos0 + (MOE_CHUNKS - 1)) // MOE_CHUNKS

        def mark(b, carry):
            be_ref[first_block + b] = e
            return carry

        lax.fori_loop(0, n_blocks, mark, 0)
        return first_block + n_blocks

    n_used = lax.fori_loop(0, N_EXPERTS, per_expert, i32(0))

    def tail(b, carry):
        be_ref[b] = i32(N_EXPERTS - 1)
        return carry

    lax.fori_loop(n_used, nb, tail, 0)
    nu_ref[0] = n_used


def _moe(x1, xs, route, counts, p, layer, tm):
    T = x1.shape[0]
    nt = T // tm
    lc = _local_rows(tm) // MOE_ALIGN
    nb = nt * lc // MOE_CHUNKS + N_EXPERTS
    i32 = jnp.int32
    cnt = counts[:, 0, ROUTER_EXPERT_LANE0:ROUTER_EXPERT_LANE0 + N_EXPERTS].astype(i32)
    ch = ((cnt + MOE_ALIGN - 1) // MOE_ALIGN).reshape(-1)
    smem = pl.BlockSpec(memory_space=pltpu.SMEM)
    src, back, block_expert, n_used = pl.pallas_call(
        functools.partial(_moe_index_kernel, nt=nt, lc=lc, nb=nb),
        out_shape=(jax.ShapeDtypeStruct((nb * MOE_CHUNKS,), i32), jax.ShapeDtypeStruct((nt * lc,), i32),
                   jax.ShapeDtypeStruct((nb,), i32), jax.ShapeDtypeStruct((1,), i32)),
        in_specs=[smem],
        out_specs=(smem, smem, smem, smem),
        scratch_shapes=[pltpu.SMEM((nt,), i32)],
        name="moe_index",
    )(ch)
    ys = _expert_mlp(xs, src.reshape(nb, 1, MOE_CHUNKS), block_expert, n_used,
                     p["w_gate"], p["w_up"], p["w_down"], layer)
    return _combine(ys, back.reshape(nt, 1, lc), route, x1, p["ln2_g"], p["ln2_b"], tm)


def _layer_params(l, w_in, b_in, conv_w, sgu_ln_g, sgu_ln_b, sgu_w, sgu_b, w_branch, w_out, ln1_g, ln1_b,
                  w_group, b_group, w_router, b_router, w_gate, w_up, w_down, ln2_g, ln2_b, tm):
    offs = [0]
    for s in IN_SIZES:
        offs.append(offs[-1] + s)
    wt, b = w_in[l].T, b_in[l]
    seg = lambda a, i: a[offs[i]:offs[i + 1]]
    fox_wt, fox_b = seg(wt, 0), seg(b, 0)
    fq_wt, fk_wt, fv_wt = (fox_wt[j * FOX_WIDTH:(j + 1) * FOX_WIDTH] for j in range(3))
    fq_b, fk_b, fv_b = (fox_b[j * FOX_WIDTH:(j + 1) * FOX_WIDTH] for j in range(3))
    wmt = jnp.concatenate([fk_wt, seg(wt, 2), seg(wt, 3), seg(wt, 4), seg(wt, 6), seg(wt, 7)], axis=0)
    bm = jnp.concatenate([fk_b, seg(b, 2), seg(b, 3), seg(b, 4), seg(b, 6), seg(b, 7)])
    n_small = IN_SIZES[1] + IN_SIZES[5]
    wst = jnp.concatenate([seg(wt, 1), seg(wt, 5)], axis=0).astype(BF16)
    ws = jnp.concatenate([wst, jnp.zeros((LANES - n_small, D_MODEL), BF16)], axis=0).T
    bs = jnp.concatenate([seg(b, 1), seg(b, 5), jnp.zeros((LANES - n_small,), F32)])
    reps = SGU_BLOCK // SGU_CHUNK
    sgu_bias = jnp.broadcast_to(sgu_b[l].T[:, :, None], (SGU_CHUNK, SGU_GROUPS, SGU_WIDTH // SGU_GROUPS))
    sgu_bias = jnp.tile(sgu_bias.reshape(SGU_CHUNK, SGU_WIDTH), (tm // SGU_CHUNK, 1))
    n_route = N_GROUPS + N_EXPERTS
    wr = jnp.concatenate([w_group[l], w_router[l], jnp.zeros((D_MODEL, LANES - n_route), F32)], axis=1)
    br = jnp.concatenate([b_group[l], b_router[l], jnp.zeros((LANES - n_route,), F32)])
    return {
        "wm": wmt.astype(BF16).T, "bm": bm.reshape(1, -1),
        "wqvt": jnp.concatenate([fq_wt, fv_wt], axis=0).astype(BF16),
        "bqvt": jnp.concatenate([fq_b, fv_b]).reshape(-1, 1),
        "ws": ws, "bs": bs.reshape(1, -1),
        "wst": wst, "bst": bs[:SM_ROWS].reshape(-1, 1),
        "conv_w": conv_w[l],
        "sgu_g": sgu_ln_g[l].reshape(1, -1), "sgu_b": sgu_ln_b[l].reshape(1, -1),
        "sgu_wt": jnp.tile(sgu_w[l], (1, reps, reps)), "sgu_bias": sgu_bias,
        "wb": w_branch[l].astype(BF16), "wo": w_out[l].astype(BF16),
        "ln1_g": ln1_g[l].reshape(1, -1), "ln1_b": ln1_b[l].reshape(1, -1),
        "wr": wr.astype(BF16), "br": br.reshape(1, -1),
        "w_gate": w_gate, "w_up": w_up, "w_down": w_down,
        "ln2_g": ln2_g[l].reshape(1, -1), "ln2_b": ln2_b[l].reshape(1, -1),
    }


def _tiles(B, S):
    T = B * S
    tm = 512 if T % 512 == 0 else 256
    tq = 512 if S % 512 == 0 else 256
    return tm, tq


def kernel(x, ln_in_g, ln_in_b, w_in, b_in, conv_w, sgu_ln_g, sgu_ln_b, sgu_w, sgu_b, w_branch, w_out,
           ln1_g, ln1_b, w_group, b_group, w_router, b_router, w_gate, w_up, w_down, ln2_g, ln2_b):
    B, S, D = x.shape
    assert D == D_MODEL and S % SGU_BLOCK == 0
    T = B * S
    tm, tq = _tiles(B, S)
    h = x.reshape(T, D)
    for l in range(DEPTH):
        p = _layer_params(l, w_in, b_in, conv_w, sgu_ln_g, sgu_ln_b, sgu_w, sgu_b, w_branch, w_out, ln1_g,
                          ln1_b, w_group, b_group, w_router, b_router, w_gate, w_up, w_down, ln2_g, ln2_b, tm)
        qt, fk, vt, mqk, mv, mo, yc, gates, sm, smt, normed = _in_projection(h, ln_in_g, ln_in_b, p, tm, l == 0)
        if l == 0:
            h = normed
        fcol, frow = _forget_cumsum(sm, smt, B, S)
        ya = _fox_attention(qt, fk, vt, fcol, frow, B, S, tq)
        yb = _mlstm(mqk, mv, mo, sm, smt, p["conv_w"], B, S)
        x1, xs, route, counts = _merge_project_route(ya, yb, yc, gates, h, p, tm)
        h = _moe(x1, xs, route, counts, p, l, tm)
    return h.reshape(B, S, D)
```
